```python
import math
import jax, jax.numpy as jnp
from jax import lax
import numpy as np

D_MODEL = 1024
BATCH = 16
SEQ = 2048
DEPTH = 2
DEC_BATCH = 32
DEC_SEQ = 4
PAST_LEN = 16384
PAGE_SIZE = 128

N_META = 16
N_A_LAYERS = (DEPTH + 1) // 2
N_B_LAYERS = DEPTH // 2
INNER = 2 * D_MODEL
H_M = 4
DH_M = INNER // H_M
CONV_W = 4
CHUNK = 64
H_A = 16
DH_A = D_MODEL // H_A
Q_BLOCK = 128
D_FF = 4 * D_MODEL
ALPHA = (2.0 * DEPTH) ** 0.25
BETA_INIT = (8.0 * DEPTH) ** -0.25
LN_EPS = 1e-5
GN_EPS = 1e-6

kernel_name = 'xlstm_stickbreak_hybrid_step'


def layer_norm(x, g, b):
    xf = x.astype(jnp.float32)
    mu = jnp.mean(xf, axis=-1, keepdims=True)
    xc = xf - mu
    var = jnp.mean(xc * xc, axis=-1, keepdims=True)
    y = xc * lax.rsqrt(var + LN_EPS) * g.astype(jnp.float32) + b.astype(jnp.float32)
    return y.astype(x.dtype)


def sq_relu_mlp(x, w1, w2):
    h = jax.nn.relu(x @ w1)
    return (h * h) @ w2


def causal_dwconv(xpad, w, bias, L):
    y = bias
    for j in range(CONV_W):
        y = y + xpad[:, j:j + L] * w[j]
    return y


def mlstm_chunk(carry, xs):
    C, n, m = carry
    q, k, v, ig, lf = xs
    L = q.shape[2]
    b = jnp.cumsum(lf, axis=-1)
    causal = jnp.tril(jnp.ones((L, L), dtype=bool))
    d = jnp.where(causal, b[..., :, None] - b[..., None, :] + ig[..., None, :], -jnp.inf)
    m_inter = b + m[..., None]
    m_t = jnp.maximum(m_inter, jnp.max(d, axis=-1))
    s = jnp.einsum('bhtd,bhsd->bhts', q, k) * jnp.exp(d - m_t[..., None])
    scale = jnp.exp(m_inter - m_t)
    num = jnp.einsum('bhts,bhsv->bhtv', s, v) + scale[..., None] * jnp.einsum('bhtd,bhdv->bhtv', q, C)
    den = jnp.sum(s, axis=-1) + scale * jnp.einsum('bhtd,bhd->bht', q, n)
    h = num / jnp.maximum(jnp.abs(den), jnp.exp(-m_t))[..., None]
    b_end = b[..., -1]
    g = b_end[..., None] - b + ig
    m_new = jnp.maximum(b_end + m, jnp.max(g, axis=-1))
    w = jnp.exp(g - m_new[..., None])
    decay = jnp.exp(b_end + m - m_new)
    C_new = decay[..., None, None] * C + jnp.einsum('bhsd,bhsv->bhdv', w[..., None] * k, v)
    n_new = decay[..., None] * n + jnp.einsum('bhs,bhsd->bhd', w, k)
    return (C_new, n_new, m_new), h


def to_chunks(t, nc, cl):
    return jnp.moveaxis(t.reshape(t.shape[:2] + (nc, cl) + t.shape[3:]), 2, 0)


def mlstm_mixer(x, conv_buf, C0, n0, m0, w_up, conv_w, conv_b, w_q, w_k, w_v, w_gate, b_gate,
                gn_g, skip, w_down, n_lead, chunk_len):
    B, L, _ = x.shape
    up = x @ w_up
    x_m, z = up[..., :INNER], up[..., INNER:]
    xpad = jnp.concatenate([conv_buf.astype(x_m.dtype), x_m], axis=1)
    x_c = jax.nn.silu(causal_dwconv(xpad, conv_w, conv_b, L))
    new_buf = xpad[:, L:]
    q = x_c @ w_q
    k = x_c @ w_k
    v = x_m @ w_v
    gates = (jnp.concatenate([q, k, v], axis=-1) @ w_gate + b_gate).astype(jnp.float32)
    ig = jnp.transpose(gates[..., :H_M], (0, 2, 1))
    lf = jnp.transpose(jax.nn.log_sigmoid(gates[..., H_M:]), (0, 2, 1))

    def heads(t):
        return t.reshape(B, L, H_M, DH_M).transpose(0, 2, 1, 3).astype(jnp.float32)

    qh, kh, vh = heads(q), heads(k) * (DH_M ** -0.5), heads(v)
    carry = (C0.astype(jnp.float32), n0.astype(jnp.float32), m0.astype(jnp.float32))
    outs = []
    if n_lead > 0:
        carry, h_lead = mlstm_chunk(carry, (qh[:, :, :n_lead], kh[:, :, :n_lead], vh[:, :, :n_lead],
                                            ig[:, :, :n_lead], lf[:, :, :n_lead]))
        outs.append(h_lead)
    rest = L - n_lead
    nc = rest // chunk_len
    xs = tuple(to_chunks(t[:, :, n_lead:], nc, chunk_len) for t in (qh, kh, vh, ig, lf))
    carry, h_rest = lax.scan(mlstm_chunk, carry, xs)
    outs.append(jnp.moveaxis(h_rest, 0, 2).reshape(B, H_M, rest, DH_M))
    h = jnp.concatenate(outs, axis=2)
    mu = jnp.mean(h, axis=-1, keepdims=True)
    hc = h - mu
    h = hc * lax.rsqrt(jnp.mean(hc * hc, axis=-1, keepdims=True) + GN_EPS)
    h = h.transpose(0, 2, 1, 3).reshape(B, L, INNER).astype(x.dtype) * gn_g
    h = h + skip * x_c
    out = (h * jax.nn.silu(z)) @ w_down
    return out, new_buf, carry


def stick_breaking(q, k, v, q_pos, k_pos, sb_bias):
    z = jnp.einsum('bqhd,bkhd->bhqk', q.astype(jnp.float32), k.astype(jnp.float32)) * (DH_A ** -0.5)
    z = z + sb_bias.astype(jnp.float32)[None, :, None, None]
    valid = k_pos[None, :] < q_pos[:, None]
    log_keep = jnp.where(valid, jax.nn.log_sigmoid(-z), 0.0)
    after = lax.cumsum(log_keep, axis=3, reverse=True) - log_keep
    a = jnp.where(valid, jnp.exp(jax.nn.log_sigmoid(z) + after), 0.0)
    return jnp.einsum('bhqk,bkhd->bqhd', a.astype(v.dtype), v)


def split_qkv(x, w_qkv):
    B, L, _ = x.shape
    q, k, v = jnp.split(x @ w_qkv, 3, axis=-1)
    return (q.reshape(B, L, H_A, DH_A), k.reshape(B, L, H_A, DH_A), v.reshape(B, L, H_A, DH_A))


def sb_prompt(x, w_qkv, w_o, sb_bias):
    B, L, _ = x.shape
    q, k, v = split_qkv(x, w_qkv)
    pos = jnp.arange(L, dtype=jnp.int32)
    o_meta = stick_breaking(q[:, :N_META], k[:, :N_META], v[:, :N_META], pos[:N_META], pos[:N_META], sb_bias)
    nb = (L - N_META) // Q_BLOCK
    qb = q[:, N_META:].reshape(B, nb, Q_BLOCK, H_A, DH_A).transpose(1, 0, 2, 3, 4)
    pb = pos[N_META:].reshape(nb, Q_BLOCK)
    o_real = lax.map(lambda a: stick_breaking(a[0], k, v, a[1], pos, sb_bias), (qb, pb))
    o_real = o_real.transpose(1, 0, 2, 3, 4).reshape(B, L - N_META, H_A, DH_A)
    o = jnp.concatenate([o_meta, o_real], axis=1).reshape(B, L, D_MODEL)
    return o @ w_o, k, v


def sb_sample(x, cache_k, cache_v, page_table, w_qkv, w_o, sb_bias):
    B, T, _ = x.shape
    q, k, v = split_qkv(x, w_qkv)
    past_len = page_table.shape[1] * PAGE_SIZE
    past_k = cache_k[page_table].reshape(B, past_len, H_A, DH_A)
    past_v = cache_v[page_table].reshape(B, past_len, H_A, DH_A)
    k_all = jnp.concatenate([past_k.astype(k.dtype), k], axis=1)
    v_all = jnp.concatenate([past_v.astype(v.dtype), v], axis=1)
    k_pos = jnp.arange(past_len + T, dtype=jnp.int32)
    q_pos = past_len + jnp.arange(T, dtype=jnp.int32)
    o = stick_breaking(q, k_all, v_all, q_pos, k_pos, sb_bias).reshape(B, T, D_MODEL)
    return o @ w_o, k, v


def setup_inputs(seed: int = 0) -> dict:
    key = jax.random.key(seed)
    ks = jax.random.split(key, 32)
    n_pages = PAST_LEN // PAGE_SIZE
    n_pool = (5 * DEC_BATCH * n_pages) // 4
    nrm = jax.random.normal
    perm = jax.random.permutation(ks[0], n_pool)[:DEC_BATCH * n_pages]
    page_table = perm.reshape(DEC_BATCH, n_pages).astype(jnp.int32)
    b_in = 0.1 * nrm(ks[1], (N_A_LAYERS, H_M))
    b_f = jnp.linspace(3.0, 6.0, H_M)[None, :] + 0.01 * nrm(ks[2], (N_A_LAYERS, H_M))
    sb_bias = jnp.linspace(-5.0, -9.0, H_A)[None, :] + 0.01 * nrm(ks[28], (N_B_LAYERS, H_A))
    return {
        'x_prompt': nrm(ks[3], (BATCH, SEQ, D_MODEL), jnp.float32),
        'x_sample': nrm(ks[4], (DEC_BATCH, DEC_SEQ, D_MODEL), jnp.float32),
        'state_C': 0.5 * nrm(ks[5], (N_A_LAYERS, DEC_BATCH, H_M, DH_M, DH_M), jnp.float32),
        'state_n': 0.5 * nrm(ks[6], (N_A_LAYERS, DEC_BATCH, H_M, DH_M), jnp.float32),
        'state_m': jax.random.uniform(ks[7], (N_A_LAYERS, DEC_BATCH, H_M), jnp.float32, 0.0, 2.0),
        'state_conv': nrm(ks[8], (N_A_LAYERS, DEC_BATCH, CONV_W - 1, INNER), jnp.float32),
        'cache_k': nrm(ks[9], (N_B_LAYERS, n_pool, PAGE_SIZE, H_A, DH_A), jnp.float32),
        'cache_v': nrm(ks[10], (N_B_LAYERS, n_pool, PAGE_SIZE, H_A, DH_A), jnp.float32),
        'page_table': page_table,
        'meta_tokens': nrm(ks[11], (N_META, D_MODEL), jnp.float32),
        'w_up': nrm(ks[12], (N_A_LAYERS, D_MODEL, 2 * INNER), jnp.float32) * D_MODEL ** -0.5,
        'conv_w': nrm(ks[13], (N_A_LAYERS, CONV_W, INNER), jnp.float32) * CONV_W ** -0.5,
        'conv_b': 0.01 * nrm(ks[14], (N_A_LAYERS, INNER), jnp.float32),
        'w_q': nrm(ks[15], (N_A_LAYERS, INNER, INNER), jnp.float32) * INNER ** -0.5,
        'w_k': nrm(ks[16], (N_A_LAYERS, INNER, INNER), jnp.float32) * INNER ** -0.5,
        'w_v': nrm(ks[17], (N_A_LAYERS, INNER, INNER), jnp.float32) * INNER ** -0.5,
        'w_gate': nrm(ks[18], (N_A_LAYERS, 3 * INNER, 2 * H_M), jnp.float32) * (0.1 * (3 * INNER) ** -0.5),
        'b_gate': jnp.concatenate([b_in, b_f], axis=-1).astype(jnp.float32),
        'gn_g': 1.0 + 0.01 * nrm(ks[19], (N_A_LAYERS, INNER), jnp.float32),
        'skip': 1.0 + 0.01 * nrm(ks[20], (N_A_LAYERS, INNER), jnp.float32),
        'w_down': nrm(ks[21], (N_A_LAYERS, INNER, D_MODEL), jnp.float32) * (INNER ** -0.5 * BETA_INIT),
        'w_qkv': nrm(ks[22], (N_B_LAYERS, D_MODEL, 3 * D_MODEL), jnp.float32) * D_MODEL ** -0.5,
        'w_o': nrm(ks[23], (N_B_LAYERS, D_MODEL, D_MODEL), jnp.float32) * (D_MODEL ** -0.5 * BETA_INIT),
        'sb_bias': sb_bias.astype(jnp.float32),
        'mlp_w1': nrm(ks[24], (DEPTH, D_MODEL, D_FF), jnp.float32) * D_MODEL ** -0.5,
        'mlp_w2': nrm(ks[25], (DEPTH, D_FF, D_MODEL), jnp.float32) * (D_FF ** -0.5 * BETA_INIT),
        'ln_g': 1.0 + 0.01 * nrm(ks[26], (DEPTH, 2, D_MODEL), jnp.float32),
        'ln_b': 0.01 * nrm(ks[27], (DEPTH, 2, D_MODEL), jnp.float32),
    }


def reference(x_prompt, x_sample, state_C, state_n, state_m, state_conv, cache_k, cache_v, page_table,
              meta_tokens, w_up, conv_w, conv_b, w_q, w_k, w_v, w_gate, b_gate, gn_g, skip, w_down,
              w_qkv, w_o, sb_bias, mlp_w1, mlp_w2, ln_g, ln_b):
    B = x_prompt.shape[0]
    meta = jnp.broadcast_to(meta_tokens.astype(x_prompt.dtype)[None], (B, N_META, D_MODEL))
    xp = jnp.concatenate([meta, x_prompt], axis=1)
    xs = x_sample
    p_C, p_n, p_m, p_conv, p_k, p_v = [], [], [], [], [], []
    s_C, s_n, s_m, s_conv, s_k, s_v = [], [], [], [], [], []
    for i in range(DEPTH):
        j = i // 2
        if i % 2 == 0:
            mlstm_w = (w_up[j], conv_w[j], conv_b[j], w_q[j], w_k[j], w_v[j], w_gate[j], b_gate[j],
                       gn_g[j], skip[j], w_down[j])
            zc = jnp.zeros((B, CONV_W - 1, INNER), xp.dtype)
            zC = jnp.zeros((B, H_M, DH_M, DH_M), jnp.float32)
            zn = jnp.zeros((B, H_M, DH_M), jnp.float32)
            zm = jnp.zeros((B, H_M), jnp.float32)
            mix_p, buf_p, (Cp, np_, mp) = mlstm_mixer(xp, zc, zC, zn, zm, *mlstm_w,
                                                     n_lead=N_META, chunk_len=CHUNK)
            mix_s, buf_s, (Cs, ns, ms) = mlstm_mixer(xs, state_conv[j], state_C[j], state_n[j], state_m[j],
                                                     *mlstm_w, n_lead=0, chunk_len=xs.shape[1])
            p_C.append(Cp); p_n.append(np_); p_m.append(mp); p_conv.append(buf_p)
            s_C.append(Cs); s_n.append(ns); s_m.append(ms); s_conv.append(buf_s)
        else:
            mix_p, kp, vp = sb_prompt(xp, w_qkv[j], w_o[j], sb_bias[j])
            mix_s, ks_, vs_ = sb_sample(xs, cache_k[j], cache_v[j], page_table, w_qkv[j], w_o[j], sb_bias[j])
            p_k.append(kp); p_v.append(vp); s_k.append(ks_); s_v.append(vs_)
        xp = layer_norm(ALPHA * xp + mix_p, ln_g[i, 0], ln_b[i, 0])
        xs = layer_norm(ALPHA * xs + mix_s, ln_g[i, 0], ln_b[i, 0])
        xp = layer_norm(ALPHA * xp + sq_relu_mlp(xp, mlp_w1[i], mlp_w2[i]), ln_g[i, 1], ln_b[i, 1])
        xs = layer_norm(ALPHA * xs + sq_relu_mlp(xs, mlp_w1[i], mlp_w2[i]), ln_g[i, 1], ln_b[i, 1])
    y_prompt = xp[:, N_META:]
    y_sample = xs
    return (y_prompt, y_sample,
            jnp.stack(p_C), jnp.stack(p_n), jnp.stack(p_m), jnp.stack(p_conv), jnp.stack(p_k), jnp.stack(p_v),
            jnp.stack(s_C), jnp.stack(s_n), jnp.stack(s_m), jnp.stack(s_conv), jnp.stack(s_k), jnp.stack(s_v))
```

```python
import functools

import jax
import jax.numpy as jnp
from jax import lax
from jax.experimental import pallas as pl
from jax.experimental.pallas import tpu as pltpu

F32 = jnp.float32
BF16 = jnp.bfloat16

LN_EPS = 1e-5
GN_EPS = 1e-6
NEG = -1e30
SEQ_BLOCK = 128
SAMPLE_ROWS = 16
GATE_LANES = 128
VMEM_LIMIT = 56 * 1024 * 1024


def _cparams(sem):
    return pltpu.CompilerParams(dimension_semantics=sem, vmem_limit_bytes=VMEM_LIMIT)


def _row_tile(n, cap):
    best = None
    for t in range(16, min(n, cap) + 1, 16):
        if n % t == 0:
            best = t
    assert best is not None, (n, cap)
    return best


def _col_tile(n, cap):
    best = None
    for t in range(128, min(n, cap) + 1, 128):
        if n % t == 0:
            best = t
    assert best is not None, (n, cap)
    return best


def _sigmoid(x):
    return 1.0 / (1.0 + jnp.exp(-x))


def _softplus(x):
    return jnp.maximum(x, 0.0) + jnp.log1p(jnp.exp(-jnp.abs(x)))


def _layer_norm(y, g, b):
    mu = jnp.mean(y, axis=-1, keepdims=True)
    yc = y - mu
    var = jnp.mean(yc * yc, axis=-1, keepdims=True)
    return yc * lax.rsqrt(var + LN_EPS) * g + b


def _proj_kernel(x_ref, w_ref, o_ref):
    o_ref[...] = jnp.dot(x_ref[...].astype(BF16), w_ref[...],
                         preferred_element_type=F32).astype(o_ref.dtype)


def _proj(x, w, out_dtype, tm_cap=1088, tn_cap=1024):
    B, L, K = x.shape
    N = w.shape[1]
    tm, tn = _row_tile(L, tm_cap), _col_tile(N, tn_cap)
    return pl.pallas_call(
        _proj_kernel,
        grid=(B, L // tm, N // tn),
        in_specs=[pl.BlockSpec((None, tm, K), lambda b, i, j: (b, i, 0)),
                  pl.BlockSpec((K, tn), lambda b, i, j: (0, j))],
        out_specs=pl.BlockSpec((None, tm, tn), lambda b, i, j: (b, i, j)),
        out_shape=jax.ShapeDtypeStruct((B, L, N), out_dtype),
        compiler_params=_cparams(("parallel", "parallel", "arbitrary")),
        name="proj",
    )(x, w)


def _qkv_m_kernel(xc_ref, xm_ref, w_ref, o_ref, *, n_qk):
    j = pl.program_id(2)

    @pl.when(j < n_qk)
    def _():
        o_ref[...] = jnp.dot(xc_ref[...], w_ref[...], preferred_element_type=F32).astype(o_ref.dtype)

    @pl.when(j >= n_qk)
    def _():
        o_ref[...] = jnp.dot(xm_ref[...], w_ref[...], preferred_element_type=F32).astype(o_ref.dtype)


def _qkv_m(xc, xm_src, w_qkv):
    B, L, I = xc.shape
    tm, tn = _row_tile(L, 1088), _col_tile(I, 1024)
    return pl.pallas_call(
        functools.partial(_qkv_m_kernel, n_qk=2 * I // tn),
        grid=(B, L // tm, 3 * I // tn),
        in_specs=[pl.BlockSpec((None, tm, I), lambda b, i, j: (b, i, 0)),
                  pl.BlockSpec((None, tm, I), lambda b, i, j: (b, i, 0)),
                  pl.BlockSpec((I, tn), lambda b, i, j: (0, j))],
        out_specs=pl.BlockSpec((None, tm, tn), lambda b, i, j: (b, i, j)),
        out_shape=jax.ShapeDtypeStruct((B, L, 3 * I), BF16),
        compiler_params=_cparams(("parallel", "parallel", "arbitrary")),
        name="qkv_m",
    )(xc, xm_src, w_qkv)


def _gates_kernel(x_ref, w_ref, b_ref, o_ref, *, n_heads):
    g = jnp.dot(x_ref[...], w_ref[...], preferred_element_type=F32) + b_ref[...]
    lane = lax.broadcasted_iota(jnp.int32, g.shape, 1)
    o_ref[...] = jnp.where(lane < n_heads, g, -_softplus(-g))


def _gates(qkv, wg, bg, n_heads):
    B, L, K = qkv.shape
    tm = _row_tile(L, 544)
    return pl.pallas_call(
        functools.partial(_gates_kernel, n_heads=n_heads),
        grid=(B, L // tm),
        in_specs=[pl.BlockSpec((None, tm, K), lambda b, i: (b, i, 0)),
                  pl.BlockSpec((K, GATE_LANES), lambda b, i: (0, 0)),
                  pl.BlockSpec((1, GATE_LANES), lambda b, i: (0, 0))],
        out_specs=pl.BlockSpec((None, tm, GATE_LANES), lambda b, i: (b, i, 0)),
        out_shape=jax.ShapeDtypeStruct((B, L, GATE_LANES), F32),
        compiler_params=_cparams(("parallel", "parallel")),
        name="gates",
    )(qkv, wg, bg)


def _conv_kernel(x_ref, w_ref, b_ref, o_ref, *, width):
    x = x_ref[...].astype(F32)
    acc = b_ref[...] + w_ref[width - 1:width, :] * x
    for j in range(width - 1):
        acc = acc + w_ref[j:j + 1, :] * pltpu.roll(x, width - 1 - j, axis=0)
    o_ref[...] = (acc * _sigmoid(acc)).astype(o_ref.dtype)


def _conv_silu(x_src, conv_w, conv_b):
    B, L, _ = x_src.shape
    width, I = conv_w.shape
    tc = _col_tile(I, 256)
    return pl.pallas_call(
        functools.partial(_conv_kernel, width=width),
        grid=(B, I // tc),
        in_specs=[pl.BlockSpec((None, L, tc), lambda b, j: (b, 0, j)),
                  pl.BlockSpec((width, tc), lambda b, j: (0, j)),
                  pl.BlockSpec((1, tc), lambda b, j: (0, j))],
        out_specs=pl.BlockSpec((None, L, tc), lambda b, j: (b, 0, j)),
        out_shape=jax.ShapeDtypeStruct((B, L, I), BF16),
        compiler_params=_cparams(("parallel", "parallel")),
        name="conv_silu",
    )(x_src, conv_w, conv_b.reshape(1, I))


def _mlstm_kernel(*refs, n_heads, n_chunks, n_pad, scale, has_state):
    if has_state:
        (q_ref, k_ref, v_ref, g_ref, gt_ref, c0_ref, n0_ref, m0_ref,
         h_ref, c_out, n_out, m_out, c_sc, n_sc, m_sc) = refs
    else:
        (q_ref, k_ref, v_ref, g_ref, gt_ref,
         h_ref, c_out, n_out, m_out, c_sc, n_sc, m_sc) = refs
    hd = pl.program_id(1)
    c = pl.program_id(2)
    L = q_ref.shape[0]

    @pl.when(c == 0)
    def _():
        if has_state:
            c_sc[...] = c0_ref[...]
            n_sc[...] = n0_ref[...]
            m_sc[...] = m0_ref[:, 0:1]
        else:
            c_sc[...] = jnp.zeros_like(c_sc)
            n_sc[...] = jnp.zeros_like(n_sc)
            m_sc[...] = jnp.zeros_like(m_sc)

    n_pad_c = jnp.where(c == 0, n_pad, 0)
    g = g_ref[...]
    row_g = lax.broadcasted_iota(jnp.int32, g.shape, 0)
    lane_g = lax.broadcasted_iota(jnp.int32, g.shape, 1)
    g = jnp.where(row_g < n_pad_c, jnp.where(lane_g < n_heads, NEG, 0.0), g)
    gt = gt_ref[...]
    sub_t = lax.broadcasted_iota(jnp.int32, gt.shape, 0)
    lane_t = lax.broadcasted_iota(jnp.int32, gt.shape, 1)
    gt = jnp.where(lane_t < n_pad_c, jnp.where(sub_t < n_heads, NEG, 0.0), gt)

    r = lax.broadcasted_iota(jnp.int32, (L, L), 0)
    s = lax.broadcasted_iota(jnp.int32, (L, L), 1)
    causal = s <= r
    tri = causal.astype(F32)
    cum_col = jnp.dot(tri, jnp.where(lane_g >= n_heads, g, 0.0),
                      preferred_element_type=F32, precision=lax.Precision.HIGHEST)
    cum_row = lax.dot_general(jnp.where(sub_t >= n_heads, gt, 0.0), tri, (((1,), (1,)), ((), ())),
                              preferred_element_type=F32, precision=lax.Precision.HIGHEST)
    b_col = jnp.sum(jnp.where(lane_g == n_heads + hd, cum_col, 0.0), axis=1, keepdims=True)
    ig_col = jnp.sum(jnp.where(lane_g == hd, g, 0.0), axis=1, keepdims=True)
    b_row = jnp.sum(jnp.where(sub_t == n_heads + hd, cum_row, 0.0), axis=0, keepdims=True)
    ig_row = jnp.sum(jnp.where(sub_t == hd, gt, 0.0), axis=0, keepdims=True)
    lf_row = jnp.sum(jnp.where(sub_t == n_heads + hd, gt, 0.0), axis=0, keepdims=True)
    b_end = jnp.sum(lf_row, axis=1, keepdims=True)

    m = m_sc[...]
    q = q_ref[...]
    k = k_ref[...]
    v = v_ref[...]

    d = jnp.where(causal, b_col - b_row + ig_row, NEG)
    m_inter = b_col + m
    m_t = jnp.maximum(m_inter, jnp.max(d, axis=1, keepdims=True))
    qk = lax.dot_general(q, k, (((1,), (1,)), ((), ())), preferred_element_type=F32)
    sw = qk * (scale * jnp.exp(d - m_t))
    sc = jnp.exp(m_inter - m_t)
    cb = c_sc[...].astype(BF16)
    num = (jnp.dot(sw.astype(BF16), v, preferred_element_type=F32)
           + sc * jnp.dot(q, cb, preferred_element_type=F32))
    den = (jnp.sum(sw, axis=1, keepdims=True)
           + sc * jnp.sum(q.astype(F32) * n_sc[...], axis=1, keepdims=True))
    hh = num / jnp.maximum(jnp.abs(den), jnp.exp(-m_t))
    mu = jnp.mean(hh, axis=1, keepdims=True)
    hc = hh - mu
    var = jnp.mean(hc * hc, axis=1, keepdims=True)
    h_ref[...] = (hc * lax.rsqrt(var + GN_EPS)).astype(h_ref.dtype)

    g_col = b_end - b_col + ig_col
    g_row = b_end - b_row + ig_row
    m_new = jnp.maximum(b_end + m, jnp.max(g_row, axis=1, keepdims=True))
    w_col = scale * jnp.exp(g_col - m_new)
    decay = jnp.exp(b_end + m - m_new)
    vw = (v.astype(F32) * w_col).astype(BF16)
    c_sc[...] = decay * c_sc[...] + lax.dot_general(k, vw, (((0,), (0,)), ((), ())),
                                                    preferred_element_type=F32)
    n_sc[...] = decay * n_sc[...] + jnp.sum(k.astype(F32) * w_col, axis=0, keepdims=True)
    m_sc[...] = m_new

    @pl.when(c == n_chunks - 1)
    def _():
        c_out[...] = c_sc[...]
        n_out[...] = n_sc[...]
        m_out[...] = jnp.broadcast_to(m_sc[...], m_out.shape)


def _mlstm(qkv, gates, n_heads, n_pad, chunk, state=None):
    B, L, I3 = qkv.shape
    I = I3 // 3
    dh = I // n_heads
    nc = L // chunk
    gates_t = jnp.swapaxes(gates[:, :, :8], 1, 2)
    has_state = state is not None

    def blk(c):
        return (c + nc - 1) % nc

    in_specs = [pl.BlockSpec((None, chunk, dh), lambda b, h, c: (b, blk(c), h)),
                pl.BlockSpec((None, chunk, dh), lambda b, h, c: (b, blk(c), n_heads + h)),
                pl.BlockSpec((None, chunk, dh), lambda b, h, c: (b, blk(c), 2 * n_heads + h)),
                pl.BlockSpec((None, chunk, GATE_LANES), lambda b, h, c: (b, blk(c), 0)),
                pl.BlockSpec((None, 8, chunk), lambda b, h, c: (b, 0, blk(c)))]
    args = [qkv, qkv, qkv, gates, gates_t]
    if has_state:
        c0, n0, m0 = state
        in_specs += [pl.BlockSpec((None, None, dh, dh), lambda b, h, c: (b, h, 0, 0)),
                     pl.BlockSpec((None, None, 1, dh), lambda b, h, c: (b, h, 0, 0)),
                     pl.BlockSpec((None, None, 1, GATE_LANES), lambda b, h, c: (b, h, 0, 0))]
        args += [c0, n0.reshape(B, n_heads, 1, dh),
                 jnp.broadcast_to(m0.reshape(B, n_heads, 1, 1), (B, n_heads, 1, GATE_LANES))]
    out_shape = (jax.ShapeDtypeStruct((B, L, I), BF16),
                 jax.ShapeDtypeStruct((B, n_heads, dh, dh), F32),
                 jax.ShapeDtypeStruct((B, n_heads, 1, dh), F32),
                 jax.ShapeDtypeStruct((B, n_heads, 1, GATE_LANES), F32))
    out_specs = (pl.BlockSpec((None, chunk, dh), lambda b, h, c: (b, blk(c), h)),
                 pl.BlockSpec((None, None, dh, dh), lambda b, h, c: (b, h, 0, 0)),
                 pl.BlockSpec((None, None, 1, dh), lambda b, h, c: (b, h, 0, 0)),
                 pl.BlockSpec((None, None, 1, GATE_LANES), lambda b, h, c: (b, h, 0, 0)))
    hn, c_new, n_new, m_new = pl.pallas_call(
        functools.partial(_mlstm_kernel, n_heads=n_heads, n_chunks=nc, n_pad=n_pad,
                          scale=float(dh) ** -0.5, has_state=has_state),
        grid=(B, n_heads, nc),
        in_specs=in_specs,
        out_specs=out_specs,
        out_shape=out_shape,
        scratch_shapes=[pltpu.VMEM((dh, dh), F32), pltpu.VMEM((1, dh), F32), pltpu.VMEM((1, 1), F32)],
        compiler_params=_cparams(("parallel", "parallel", "arbitrary")),
        name="mlstm",
    )(*args)
    return hn, c_new, n_new[:, :, 0, :], m_new[:, :, 0, 0]


def _post_ln_kernel(*refs, gated, alpha):
    if gated:
        hn_ref, xc_ref, z_ref, gn_ref, sk_ref, res_ref, w_ref, g_ref, b_ref, o_ref = refs
        z = z_ref[...].astype(F32)
        hcomb = hn_ref[...].astype(F32) * gn_ref[...] + sk_ref[...] * xc_ref[...].astype(F32)
        lhs = (hcomb * (z * _sigmoid(z))).astype(BF16)
    else:
        lhs_ref, res_ref, w_ref, g_ref, b_ref, o_ref = refs
        lhs = lhs_ref[...]
    mix = jnp.dot(lhs, w_ref[...], preferred_element_type=F32)
    y = alpha * res_ref[...] + mix
    o_ref[...] = _layer_norm(y, g_ref[...], b_ref[...])


def _post_ln(lhs_args, res, w, ln_g, ln_b, alpha, rows, gated):
    B = res.shape[0]
    K, D = w.shape
    tm = _row_tile(rows, 544)
    row = lambda b, i: (b, i, 0)
    const = lambda b, i: (0, 0)
    if gated:
        hn, xc, z_src, gn_g, skip = lhs_args
        nz = z_src.shape[2] // K - 1
        in_specs = [pl.BlockSpec((None, tm, K), row),
                    pl.BlockSpec((None, tm, K), row),
                    pl.BlockSpec((None, tm, K), lambda b, i: (b, i, nz)),
                    pl.BlockSpec((1, K), const),
                    pl.BlockSpec((1, K), const)]
        args = [hn, xc, z_src, gn_g.reshape(1, K), skip.reshape(1, K)]
    else:
        in_specs = [pl.BlockSpec((None, tm, K), row)]
        args = [lhs_args]
    in_specs += [pl.BlockSpec((None, tm, D), row),
                 pl.BlockSpec((K, D), const),
                 pl.BlockSpec((1, D), const),
                 pl.BlockSpec((1, D), const)]
    args += [res, w, ln_g.reshape(1, D), ln_b.reshape(1, D)]
    return pl.pallas_call(
        functools.partial(_post_ln_kernel, gated=gated, alpha=alpha),
        grid=(B, rows // tm),
        in_specs=in_specs,
        out_specs=pl.BlockSpec((None, tm, D), row),
        out_shape=jax.ShapeDtypeStruct((B, rows, D), F32),
        compiler_params=_cparams(("parallel", "parallel")),
        name="post_ln_gated" if gated else "post_ln",
    )(*args)


def _mlp_kernel(x_ref, w1_ref, w2_ref, g_ref, b_ref, o_ref, acc_ref, *, alpha, n_chunks):
    c = pl.program_id(2)
    hid = jnp.dot(x_ref[...].astype(BF16), w1_ref[...], preferred_element_type=F32)
    hid = jnp.maximum(hid, 0.0)
    part = jnp.dot((hid * hid).astype(BF16), w2_ref[...], preferred_element_type=F32)

    @pl.when(c == 0)
    def _():
        acc_ref[...] = part

    @pl.when(c > 0)
    def _():
        acc_ref[...] += part

    @pl.when(c == n_chunks - 1)
    def _():
        y = alpha * x_ref[...] + acc_ref[...]
        o_ref[...] = _layer_norm(y, g_ref[...], b_ref[...])


def _mlp_ln(x, w1, w2, ln_g, ln_b, alpha, rows):
    B, _, D = x.shape
    F = w1.shape[1]
    tm, tf = _row_tile(rows, 1088), _col_tile(F, 1024)
    nc = F // tf
    return pl.pallas_call(
        functools.partial(_mlp_kernel, alpha=alpha, n_chunks=nc),
        grid=(B, rows // tm, nc),
        in_specs=[pl.BlockSpec((None, tm, D), lambda b, i, c: (b, i, 0)),
                  pl.BlockSpec((D, tf), lambda b, i, c: (0, c)),
                  pl.BlockSpec((tf, D), lambda b, i, c: (c, 0)),
                  pl.BlockSpec((1, D), lambda b, i, c: (0, 0)),
                  pl.BlockSpec((1, D), lambda b, i, c: (0, 0))],
        out_specs=pl.BlockSpec((None, tm, D), lambda b, i, c: (b, i, 0)),
        out_shape=jax.ShapeDtypeStruct((B, rows, D), F32),
        scratch_shapes=[pltpu.VMEM((tm, D), F32)],
        compiler_params=_cparams(("parallel", "parallel", "arbitrary")),
        name="mlp_ln",
    )(x, w1, w2, ln_g.reshape(1, D), ln_b.reshape(1, D))


def _qkv_a_kernel(x_ref, wq_ref, wk_ref, wv_ref, q_ref, k_ref, v_ref, pk_ref, pv_ref, *,
                  seq, n_meta, q_scale):
    x = x_ref[...].astype(BF16)
    L = x.shape[0]
    q = jnp.dot(x, wq_ref[...], preferred_element_type=F32)
    q_ref[...] = (q * q_scale).astype(q_ref.dtype)
    for w_ref, o_ref, p_ref in ((wk_ref, k_ref, pk_ref), (wv_ref, v_ref, pv_ref)):
        y = jnp.dot(x, w_ref[...], preferred_element_type=F32)
        o_ref[...] = y.astype(o_ref.dtype)
        p_ref[0:n_meta, :] = y[L - n_meta:L, :]
        p_ref[n_meta:n_meta + seq, :] = y[0:seq, :]


def _qkv_a(x, w_qkv, seq, n_meta, q_scale):
    B, L, D = x.shape
    tn = _col_tile(D, 256)
    nj = D // tn
    xs = pl.BlockSpec((None, L, D), lambda b, j: (b, 0, 0))
    os = pl.BlockSpec((None, L, tn), lambda b, j: (b, 0, j))
    ps = pl.BlockSpec((None, n_meta + seq, tn), lambda b, j: (b, 0, j))
    return pl.pallas_call(
        functools.partial(_qkv_a_kernel, seq=seq, n_meta=n_meta, q_scale=q_scale),
        grid=(B, nj),
        in_specs=[xs,
                  pl.BlockSpec((D, tn), lambda b, j: (0, j)),
                  pl.BlockSpec((D, tn), lambda b, j: (0, nj + j)),
                  pl.BlockSpec((D, tn), lambda b, j: (0, 2 * nj + j))],
        out_specs=(os, os, os, ps, ps),
        out_shape=(jax.ShapeDtypeStruct((B, L, D), BF16),) * 3
        + (jax.ShapeDtypeStruct((B, n_meta + seq, D), F32),) * 2,
        compiler_params=_cparams(("parallel", "arbitrary")),
        name="qkv_a",
    )(x, w_qkv, w_qkv, w_qkv)


def _sb_block(z, valid, carry, su):
    lk = -_softplus(z)
    if valid is not None:
        lk = jnp.where(valid, lk, 0.0)
    after = carry + jnp.dot(lk.astype(BF16), su, preferred_element_type=F32)
    a = jnp.exp(z + lk + after)
    if valid is not None:
        a = jnp.where(valid, a, 0.0)
    return a, carry + jnp.sum(lk, axis=1, keepdims=True)


def _strict_upper(n):
    j = lax.broadcasted_iota(jnp.int32, (n, n), 0)
    s = lax.broadcasted_iota(jnp.int32, (n, n), 1)
    return (j > s).astype(BF16)


def _attn_kernel(q_ref, k_ref, v_ref, bias_ref, o_ref, *, n_blocks, n_pad, dh):
    i = pl.program_id(2)
    S = SEQ_BLOCK
    q = q_ref[...]
    lane = lax.broadcasted_iota(jnp.int32, q.shape, 1)
    first = lane < dh
    zero = jnp.zeros_like(q)
    q_tall = jnp.concatenate([jnp.where(first, q, zero), jnp.where(first, zero, q)], axis=0)
    bias = bias_ref[...]
    su = _strict_upper(S)
    row = lax.broadcasted_iota(jnp.int32, (2 * S, S), 0) & (S - 1)
    col = lax.broadcasted_iota(jnp.int32, (2 * S, S), 1)

    def body(t, state):
        carry, acc = state
        j = i - t
        start = pl.multiple_of(((j + n_blocks - 1) % n_blocks) * S, S)
        kb = k_ref[pl.ds(start, S), :]
        vb = v_ref[pl.ds(start, S), :]
        z = lax.dot_general(q_tall, kb, (((1,), (1,)), ((), ())), preferred_element_type=F32) + bias
        valid = (col < row + t * S) & (col >= jnp.where(j == 0, n_pad, 0))
        a, carry = _sb_block(z, valid, carry, su)
        a = a.astype(BF16)
        a_wide = jnp.concatenate([a[:S], a[S:]], axis=1)
        v_tall = jnp.concatenate([jnp.where(first, vb, zero), jnp.where(first, zero, vb)], axis=0)
        acc = acc + jnp.dot(a_wide, v_tall, preferred_element_type=F32)
        return carry, acc

    init = (jnp.zeros((2 * S, 1), F32), jnp.zeros((S, 2 * dh), F32))
    _, acc = lax.fori_loop(0, i + 1, body, init)
    o_ref[...] = acc.astype(o_ref.dtype)


def _attn(q, k, v, sb_bias, n_pad):
    B, L, D = q.shape
    H = sb_bias.shape[0]
    dh = D // H
    assert 2 * dh == SEQ_BLOCK
    S = SEQ_BLOCK
    nb = L // S
    bias = jnp.broadcast_to(sb_bias.astype(F32).reshape(H // 2, 2, 1, 1), (H // 2, 2, S, S))
    bias = bias.reshape(H // 2, 2 * S, S)
    return pl.pallas_call(
        functools.partial(_attn_kernel, n_blocks=nb, n_pad=n_pad, dh=dh),
        grid=(B, H // 2, nb),
        in_specs=[pl.BlockSpec((None, S, 2 * dh), lambda b, p, i: (b, (i + nb - 1) % nb, p)),
                  pl.BlockSpec((None, L, 2 * dh), lambda b, p, i: (b, 0, p)),
                  pl.BlockSpec((None, L, 2 * dh), lambda b, p, i: (b, 0, p)),
                  pl.BlockSpec((None, 2 * S, S), lambda b, p, i: (p, 0, 0))],
        out_specs=pl.BlockSpec((None, S, 2 * dh), lambda b, p, i: (b, (i + nb - 1) % nb, p)),
        out_shape=jax.ShapeDtypeStruct((B, L, D), BF16),
        compiler_params=_cparams(("parallel", "parallel", "arbitrary")),
        name="sb_attn",
    )(q, k, v, bias)


def _sattn_kernel(pt_ref, qbd_ref, bias_ref, kn_ref, vn_ref, *refs, n_groups, group, n_heads, dh, n_new):
    del pt_ref
    k_refs = refs[:group]
    v_refs = refs[group:2 * group]
    o_ref, acc_ref, carry_ref = refs[2 * group:]
    g = pl.program_id(1)
    qbd = qbd_ref[...]
    bias = bias_ref[...]
    R, S = bias.shape
    su = _strict_upper(S)

    def block(kb, vb, valid):
        z = lax.dot_general(qbd, kb, (((1,), (1,)), ((), ())), preferred_element_type=F32) + bias
        a, carry = _sb_block(z, valid, carry_ref[...], su)
        carry_ref[...] = carry
        acc_ref[...] += jnp.dot(a.astype(BF16), vb, preferred_element_type=F32)

    @pl.when(g == 0)
    def _():
        acc_ref[...] = jnp.zeros_like(acc_ref)
        carry_ref[...] = jnp.zeros_like(carry_ref)
        t = lax.broadcasted_iota(jnp.int32, (R, S), 0) // n_heads
        s = lax.broadcasted_iota(jnp.int32, (R, S), 1)
        block(kn_ref[...], vn_ref[...], s < t)

    for idx in range(group):
        block(k_refs[idx][...].astype(BF16), v_refs[idx][...].astype(BF16), None)

    @pl.when(g == n_groups - 1)
    def _():
        acc = acc_ref[...]
        row_h = lax.broadcasted_iota(jnp.int32, acc.shape, 0) % n_heads
        col_h = lax.broadcasted_iota(jnp.int32, acc.shape, 1) // dh
        own = jnp.where(row_h == col_h, acc, 0.0)
        o_ref[...] = jnp.sum(own.reshape(n_new, n_heads, acc.shape[1]), axis=1)


def _sattn(q, k_new, v_new, cache_k, cache_v, page_table, sb_bias, q_scale):
    B, T, D = q.shape
    H = sb_bias.shape[0]
    dh = D // H
    n_pool, S = cache_k.shape[0], cache_k.shape[1]
    n_pages = page_table.shape[1]
    group = 4 if n_pages % 4 == 0 else 1
    n_groups = n_pages // group
    R = T * H
    head_of_col = jnp.arange(D, dtype=jnp.int32) // dh
    own = head_of_col[None, None, :] == jnp.arange(H, dtype=jnp.int32)[None, :, None]
    qbd = jnp.where(own[:, None], (q * q_scale)[:, :, None, :], 0.0).reshape(B, R, D).astype(BF16)
    bias = jnp.broadcast_to(jnp.tile(sb_bias.astype(F32), T)[:, None], (R, S))
    pad = ((0, 0), (0, S - T), (0, 0))
    kn = jnp.pad(k_new, pad).astype(BF16)
    vn = jnp.pad(v_new, pad).astype(BF16)
    ck = cache_k.reshape(n_pool, S, D)
    cv = cache_v.reshape(n_pool, S, D)

    def page_spec(idx):
        return pl.BlockSpec((None, S, D),
                            lambda b, g, pt: (pt[b, n_pages - 1 - (g * group + idx)], 0, 0))

    grid_spec = pltpu.PrefetchScalarGridSpec(
        num_scalar_prefetch=1,
        grid=(B, n_groups),
        in_specs=[pl.BlockSpec((None, R, D), lambda b, g, pt: (b, 0, 0)),
                  pl.BlockSpec((R, S), lambda b, g, pt: (0, 0)),
                  pl.BlockSpec((None, S, D), lambda b, g, pt: (b, 0, 0)),
                  pl.BlockSpec((None, S, D), lambda b, g, pt: (b, 0, 0))]
        + [page_spec(idx) for idx in range(group)] * 2,
        out_specs=pl.BlockSpec((None, T, D), lambda b, g, pt: (b, 0, 0)),
        scratch_shapes=[pltpu.VMEM((R, D), F32), pltpu.VMEM((R, 1), F32)],
    )
    return pl.pallas_call(
        functools.partial(_sattn_kernel, n_groups=n_groups, group=group, n_heads=H, dh=dh, n_new=T),
        grid_spec=grid_spec,
        out_shape=jax.ShapeDtypeStruct((B, T, D), F32),
        compiler_params=_cparams(("parallel", "arbitrary")),
        name="sb_attn_paged",
    )(page_table, qbd, bias, kn, vn, *([ck] * group), *([cv] * group))


def _mlstm_layer(x, rows_out, n_pad, chunk, conv_hist, state, wts, ln, alpha):
    w_up, conv_w, conv_b, w_qkv, wg, bg, gn_g, skip, w_down, n_heads = wts
    I = conv_w.shape[1]
    up = _proj(x, w_up, BF16)
    if conv_hist is None:
        conv_src = up
    else:
        nh = conv_hist.shape[1]
        conv_src = jnp.concatenate([up[:, :n_pad - nh, :I], conv_hist.astype(BF16), up[:, n_pad:, :I]], axis=1)
    xc = _conv_silu(conv_src, conv_w, conv_b)
    qkv = _qkv_m(xc, up, w_qkv)
    gates = _gates(qkv, wg, bg, n_heads)
    hn, c_new, n_new, m_new = _mlstm(qkv, gates, n_heads, n_pad, chunk, state)
    x1 = _post_ln((hn, xc, up, gn_g, skip), x, w_down, ln[0], ln[1], alpha, rows_out, gated=True)
    return x1, up, (c_new, n_new, m_new)


def kernel(x_prompt, x_sample, state_C, state_n, state_m, state_conv, cache_k, cache_v, page_table,
           meta_tokens, w_up, conv_w, conv_b, w_q, w_k, w_v, w_gate, b_gate, gn_g, skip, w_down,
           w_qkv, w_o, sb_bias, mlp_w1, mlp_w2, ln_g, ln_b):
    B, SEQ, D = x_prompt.shape
    BD, T, _ = x_sample.shape
    depth = mlp_w1.shape[0]
    n_meta = meta_tokens.shape[0]
    I = conv_w.shape[2]
    width = conv_w.shape[1]
    HM = b_gate.shape[1] // 2
    HA = sb_bias.shape[1]
    alpha = (2.0 * depth) ** 0.25
    q_scale = float(D // HA) ** -0.5
    S = SEQ_BLOCK
    assert SEQ % S == 0 and n_meta <= S and T + width - 1 <= SAMPLE_ROWS

    n_pad_p = S - n_meta
    LP = SEQ + S
    xp = jnp.concatenate([x_prompt, jnp.zeros((B, n_pad_p, D), F32),
                          jnp.broadcast_to(meta_tokens.astype(F32)[None], (B, n_meta, D))], axis=1)
    n_pad_s = SAMPLE_ROWS - T
    xs = jnp.pad(x_sample, ((0, 0), (n_pad_s, 0), (0, 0)))

    outs = {name: [] for name in ("p_C", "p_n", "p_m", "p_conv", "p_k", "p_v",
                                  "s_C", "s_n", "s_m", "s_conv", "s_k", "s_v")}
    for i in range(depth):
        j = i // 2
        last = i == depth - 1
        rows_p = SEQ if last else LP
        lng, lnb = ln_g[i], ln_b[i]
        w1, w2 = mlp_w1[i].astype(BF16), mlp_w2[i].astype(BF16)
        if i % 2 == 0:
            wg = jnp.pad(w_gate[j], ((0, 0), (0, GATE_LANES - 2 * HM))).astype(BF16)
            bg = jnp.pad(b_gate[j], (0, GATE_LANES - 2 * HM)).reshape(1, GATE_LANES).astype(F32)
            wts = (w_up[j].astype(BF16), conv_w[j], conv_b[j],
                   jnp.concatenate([w_q[j], w_k[j], w_v[j]], axis=1).astype(BF16), wg, bg,
                   gn_g[j], skip[j], w_down[j].astype(BF16), HM)
            xp1, up_p, (Cp, np_, mp) = _mlstm_layer(xp, rows_p, n_pad_p, S, None, None, wts,
                                                    (lng[0], lnb[0]), alpha)
            xs1, up_s, (Cs, ns, ms) = _mlstm_layer(xs, SAMPLE_ROWS, n_pad_s, SAMPLE_ROWS, state_conv[j],
                                                   (state_C[j], state_n[j], state_m[j]), wts,
                                                   (lng[0], lnb[0]), alpha)
            outs["p_C"].append(Cp); outs["p_n"].append(np_); outs["p_m"].append(mp)
            outs["p_conv"].append(up_p[:, SEQ - (width - 1):SEQ, :I].astype(F32))
            outs["s_C"].append(Cs); outs["s_n"].append(ns); outs["s_m"].append(ms)
            hist = jnp.concatenate([state_conv[j], up_s[:, n_pad_s:, :I].astype(F32)], axis=1)
            outs["s_conv"].append(hist[:, T:])
        else:
            wqkv = w_qkv[j].astype(BF16)
            wo = w_o[j].astype(BF16)
            q, k, v, pk, pv = _qkv_a(xp, wqkv, SEQ, n_meta, q_scale)
            o = _attn(q, k, v, sb_bias[j], n_pad_p)
            xp1 = _post_ln(o, xp, wo, lng[0], lnb[0], alpha, rows_p, gated=False)
            outs["p_k"].append(pk.reshape(B, n_meta + SEQ, HA, D // HA))
            outs["p_v"].append(pv.reshape(B, n_meta + SEQ, HA, D // HA))
            qkv_s = _proj(xs, wqkv, F32)[:, n_pad_s:]
            qs, ks, vs = qkv_s[..., :D], qkv_s[..., D:2 * D], qkv_s[..., 2 * D:]
            o_s = _sattn(qs, ks, vs, cache_k[j], cache_v[j], page_table, sb_bias[j], q_scale)
            o_s = jnp.pad(o_s, ((0, 0), (n_pad_s, 0), (0, 0))).astype(BF16)
            xs1 = _post_ln(o_s, xs, wo, lng[0], lnb[0], alpha, SAMPLE_ROWS, gated=False)
            outs["s_k"].append(ks.reshape(BD, T, HA, D // HA))
            outs["s_v"].append(vs.reshape(BD, T, HA, D // HA))
        xp = _mlp_ln(xp1, w1, w2, lng[1], lnb[1], alpha, rows_p)
        xs = _mlp_ln(xs1, w1, w2, lng[1], lnb[1], alpha, SAMPLE_ROWS)
    y_prompt = xp[:, :SEQ]
    y_sample = xs[:, n_pad_s:]
    st = lambda name: jnp.stack(outs[name])
    return (y_prompt, y_sample, st("p_C"), st("p_n"), st("p_m"), st("p_conv"), st("p_k"), st("p_v"),
            st("s_C"), st("s_n"), st("s_m"), st("s_conv"), st("s_k"), st("s_v"))
```

```python
import functools

import jax
import jax.numpy as jnp
from jax import lax
from jax.experimental import pallas as pl
from jax.experimental.pallas import tpu as pltpu

F32 = jnp.float32
BF16 = jnp.bfloat16

LN_EPS = 1e-5
GN_EPS = 1e-6
NEG = -1e30
LOG2E = 1.4426950408889634
SEQ_BLOCK = 128
SAMPLE_ROWS = 16
GATE_LANES = 128
ATTN_Q_BLOCKS = 4
VMEM_LIMIT = 56 * 1024 * 1024


def _cparams(sem):
    return pltpu.CompilerParams(dimension_semantics=sem, vmem_limit_bytes=VMEM_LIMIT)


def _row_tile(n, cap):
    best = None
    for t in range(16, min(n, cap) + 1, 16):
        if n % t == 0:
            best = t
    assert best is not None, (n, cap)
    return best


def _col_tile(n, cap):
    best = None
    for t in range(128, min(n, cap) + 1, 128):
        if n % t == 0:
            best = t
    assert best is not None, (n, cap)
    return best


def _sigmoid(x):
    return 1.0 / (1.0 + jnp.exp(-x))


def _softplus(x):
    return jnp.maximum(x, 0.0) + jnp.log1p(jnp.exp(-jnp.abs(x)))


def _layer_norm(y, g, b):
    mu = jnp.mean(y, axis=-1, keepdims=True)
    yc = y - mu
    var = jnp.mean(yc * yc, axis=-1, keepdims=True)
    return yc * lax.rsqrt(var + LN_EPS) * g + b


def _proj_kernel(x_ref, w_ref, o_ref):
    o_ref[...] = jnp.dot(x_ref[...].astype(BF16), w_ref[...],
                         preferred_element_type=F32).astype(o_ref.dtype)


def _proj(x, w, out_dtype, tm_cap=1088, tn_cap=1024):
    B, L, K = x.shape
    N = w.shape[1]
    tm, tn = _row_tile(L, tm_cap), _col_tile(N, tn_cap)
    return pl.pallas_call(
        _proj_kernel,
        grid=(B, L // tm, N // tn),
        in_specs=[pl.BlockSpec((None, tm, K), lambda b, i, j: (b, i, 0)),
                  pl.BlockSpec((K, tn), lambda b, i, j: (0, j))],
        out_specs=pl.BlockSpec((None, tm, tn), lambda b, i, j: (b, i, j)),
        out_shape=jax.ShapeDtypeStruct((B, L, N), out_dtype),
        compiler_params=_cparams(("parallel", "parallel", "arbitrary")),
        name="proj",
    )(x, w)


def _qkv_m_kernel(xc_ref, xm_ref, w_ref, o_ref, *, n_qk):
    j = pl.program_id(2)

    @pl.when(j < n_qk)
    def _():
        o_ref[...] = jnp.dot(xc_ref[...], w_ref[...], preferred_element_type=F32).astype(o_ref.dtype)

    @pl.when(j >= n_qk)
    def _():
        o_ref[...] = jnp.dot(xm_ref[...], w_ref[...], preferred_element_type=F32).astype(o_ref.dtype)


def _qkv_m(xc, xm_src, w_qkv):
    B, L, I = xc.shape
    tm, tn = _row_tile(L, 1088), _col_tile(I, 1024)
    return pl.pallas_call(
        functools.partial(_qkv_m_kernel, n_qk=2 * I // tn),
        grid=(B, L // tm, 3 * I // tn),
        in_specs=[pl.BlockSpec((None, tm, I), lambda b, i, j: (b, i, 0)),
                  pl.BlockSpec((None, tm, I), lambda b, i, j: (b, i, 0)),
                  pl.BlockSpec((I, tn), lambda b, i, j: (0, j))],
        out_specs=pl.BlockSpec((None, tm, tn), lambda b, i, j: (b, i, j)),
        out_shape=jax.ShapeDtypeStruct((B, L, 3 * I), BF16),
        compiler_params=_cparams(("parallel", "parallel", "arbitrary")),
        name="qkv_m",
    )(xc, xm_src, w_qkv)


def _gates_kernel(x_ref, w_ref, b_ref, o_ref, *, n_heads):
    g = jnp.dot(x_ref[...], w_ref[...], preferred_element_type=F32) + b_ref[...]
    lane = lax.broadcasted_iota(jnp.int32, g.shape, 1)
    o_ref[...] = jnp.where(lane < n_heads, g, -_softplus(-g))


def _gates(qkv, wg, bg, n_heads):
    B, L, K = qkv.shape
    tm = _row_tile(L, 544)
    return pl.pallas_call(
        functools.partial(_gates_kernel, n_heads=n_heads),
        grid=(B, L // tm),
        in_specs=[pl.BlockSpec((None, tm, K), lambda b, i: (b, i, 0)),
                  pl.BlockSpec((K, GATE_LANES), lambda b, i: (0, 0)),
                  pl.BlockSpec((1, GATE_LANES), lambda b, i: (0, 0))],
        out_specs=pl.BlockSpec((None, tm, GATE_LANES), lambda b, i: (b, i, 0)),
        out_shape=jax.ShapeDtypeStruct((B, L, GATE_LANES), F32),
        compiler_params=_cparams(("parallel", "parallel")),
        name="gates",
    )(qkv, wg, bg)


def _conv_kernel(x_ref, w_ref, b_ref, o_ref, *, width, zero_rows):
    x = x_ref[...].astype(F32)
    if zero_rows is not None:
        row = lax.broadcasted_iota(jnp.int32, x.shape, 0)
        x = jnp.where((row >= zero_rows[0]) & (row < zero_rows[1]), 0.0, x)
    acc = b_ref[...] + w_ref[width - 1:width, :] * x
    for j in range(width - 1):
        acc = acc + w_ref[j:j + 1, :] * pltpu.roll(x, width - 1 - j, axis=0)
    o_ref[...] = (acc * _sigmoid(acc)).astype(o_ref.dtype)


def _conv_silu(x_src, conv_w, conv_b, zero_rows):
    B, L, _ = x_src.shape
    width, I = conv_w.shape
    tc = _col_tile(I, 256)
    return pl.pallas_call(
        functools.partial(_conv_kernel, width=width, zero_rows=zero_rows),
        grid=(B, I // tc),
        in_specs=[pl.BlockSpec((None, L, tc), lambda b, j: (b, 0, j)),
                  pl.BlockSpec((width, tc), lambda b, j: (0, j)),
                  pl.BlockSpec((1, tc), lambda b, j: (0, j))],
        out_specs=pl.BlockSpec((None, L, tc), lambda b, j: (b, 0, j)),
        out_shape=jax.ShapeDtypeStruct((B, L, I), BF16),
        compiler_params=_cparams(("parallel", "parallel")),
        name="conv_silu",
    )(x_src, conv_w, conv_b.reshape(1, I))


def _mlstm_kernel(*refs, n_heads, n_chunks, n_pad, scale, has_state):
    if has_state:
        (q_ref, k_ref, v_ref, g_ref, gt_ref, c0_ref, n0_ref, m0_ref,
         h_ref, c_out, n_out, m_out, c_sc, n_sc, m_sc) = refs
    else:
        (q_ref, k_ref, v_ref, g_ref, gt_ref,
         h_ref, c_out, n_out, m_out, c_sc, n_sc, m_sc) = refs
    c = pl.program_id(1)
    L = q_ref.shape[0]
    dh = q_ref.shape[1] // n_heads

    @pl.when(c == 0)
    def _():
        if has_state:
            c_sc[...] = c0_ref[...]
            n_sc[...] = n0_ref[...]
            m_sc[...] = m0_ref[...]
        else:
            c_sc[...] = jnp.zeros_like(c_sc)
            n_sc[...] = jnp.zeros_like(n_sc)
            m_sc[...] = jnp.zeros_like(m_sc)

    n_pad_c = jnp.where(c == 0, n_pad, 0)
    g = g_ref[...]
    row_g = lax.broadcasted_iota(jnp.int32, g.shape, 0)
    lane_g = lax.broadcasted_iota(jnp.int32, g.shape, 1)
    g = jnp.where(row_g < n_pad_c, jnp.where(lane_g < n_heads, NEG, 0.0), g)
    gt = gt_ref[...]
    sub_t = lax.broadcasted_iota(jnp.int32, gt.shape, 0)
    lane_t = lax.broadcasted_iota(jnp.int32, gt.shape, 1)
    gt = jnp.where(lane_t < n_pad_c, jnp.where(sub_t < n_heads, NEG, 0.0), gt)

    r = lax.broadcasted_iota(jnp.int32, (L, L), 0)
    s = lax.broadcasted_iota(jnp.int32, (L, L), 1)
    causal = s <= r
    tri = causal.astype(F32)
    cum_col = jnp.dot(tri, jnp.where(lane_g >= n_heads, g, 0.0),
                      preferred_element_type=F32, precision=lax.Precision.HIGHEST)
    cum_row = lax.dot_general(jnp.where(sub_t >= n_heads, gt, 0.0), tri, (((1,), (1,)), ((), ())),
                              preferred_element_type=F32, precision=lax.Precision.HIGHEST)

    for hd in range(n_heads):
        cols = slice(hd * dh, (hd + 1) * dh)
        b_col = cum_col[:, n_heads + hd:n_heads + hd + 1]
        ig_col = g[:, hd:hd + 1]
        b_row = cum_row[n_heads + hd:n_heads + hd + 1, :]
        ig_row = gt[hd:hd + 1, :]
        b_end = jnp.sum(gt[n_heads + hd:n_heads + hd + 1, :], axis=1, keepdims=True)
        m = m_sc[hd][:, 0:1]
        q = q_ref[:, cols]
        k = k_ref[:, cols]
        v = v_ref[:, cols]

        d = jnp.where(causal, b_col - b_row + ig_row, NEG)
        m_inter = b_col + m
        m_t = jnp.maximum(m_inter, jnp.max(d, axis=1, keepdims=True))
        qk = lax.dot_general(q, k, (((1,), (1,)), ((), ())), preferred_element_type=F32)
        sw = qk * (scale * jnp.exp(d - m_t))
        sc = jnp.exp(m_inter - m_t)
        cb = c_sc[hd].astype(BF16)
        num = (jnp.dot(sw.astype(BF16), v, preferred_element_type=F32)
               + sc * jnp.dot(q, cb, preferred_element_type=F32))
        den = (jnp.sum(sw, axis=1, keepdims=True)
               + sc * jnp.sum(q.astype(F32) * n_sc[hd], axis=1, keepdims=True))
        hh = num * (1.0 / jnp.maximum(jnp.abs(den), jnp.exp(-m_t)))
        mu = jnp.mean(hh, axis=1, keepdims=True)
        hc = hh - mu
        var = jnp.mean(hc * hc, axis=1, keepdims=True)
        h_ref[:, cols] = (hc * lax.rsqrt(var + GN_EPS)).astype(h_ref.dtype)

        g_col = b_end - b_col + ig_col
        g_row = b_end - b_row + ig_row
        m_new = jnp.maximum(b_end + m, jnp.max(g_row, axis=1, keepdims=True))
        w_col = scale * jnp.exp(g_col - m_new)
        decay = jnp.exp(b_end + m - m_new)
        vw = (v.astype(F32) * w_col).astype(BF16)
        c_sc[hd] = decay * c_sc[hd] + lax.dot_general(k, vw, (((0,), (0,)), ((), ())),
                                                      preferred_element_type=F32)
        n_sc[hd] = decay * n_sc[hd] + jnp.sum(k.astype(F32) * w_col, axis=0, keepdims=True)
        m_sc[hd] = jnp.broadcast_to(m_new, m_sc.shape[1:])

    @pl.when(c == n_chunks - 1)
    def _():
        c_out[...] = c_sc[...]
        n_out[...] = n_sc[...]
        m_out[...] = m_sc[...]


def _mlstm(qkv, gates, n_heads, n_pad, chunk, state=None):
    B, L, I3 = qkv.shape
    I = I3 // 3
    dh = I // n_heads
    nc = L // chunk
    assert 2 * n_heads <= 8
    gates_t = jnp.swapaxes(gates[:, :, :8], 1, 2)
    has_state = state is not None

    def blk(c):
        return (c + nc - 1) % nc

    state_specs = [pl.BlockSpec((None, n_heads, dh, dh), lambda b, c: (b, 0, 0, 0)),
                   pl.BlockSpec((None, n_heads, 1, dh), lambda b, c: (b, 0, 0, 0)),
                   pl.BlockSpec((None, n_heads, 1, GATE_LANES), lambda b, c: (b, 0, 0, 0))]
    in_specs = [pl.BlockSpec((None, chunk, I), lambda b, c: (b, blk(c), 0)),
                pl.BlockSpec((None, chunk, I), lambda b, c: (b, blk(c), 1)),
                pl.BlockSpec((None, chunk, I), lambda b, c: (b, blk(c), 2)),
                pl.BlockSpec((None, chunk, GATE_LANES), lambda b, c: (b, blk(c), 0)),
                pl.BlockSpec((None, 8, chunk), lambda b, c: (b, 0, blk(c)))]
    args = [qkv, qkv, qkv, gates, gates_t]
    if has_state:
        c0, n0, m0 = state
        in_specs += state_specs
        args += [c0, n0.reshape(B, n_heads, 1, dh),
                 jnp.broadcast_to(m0.reshape(B, n_heads, 1, 1), (B, n_heads, 1, GATE_LANES))]
    out_shape = (jax.ShapeDtypeStruct((B, L, I), BF16),
                 jax.ShapeDtypeStruct((B, n_heads, dh, dh), F32),
                 jax.ShapeDtypeStruct((B, n_heads, 1, dh), F32),
                 jax.ShapeDtypeStruct((B, n_heads, 1, GATE_LANES), F32))
    out_specs = [pl.BlockSpec((None, chunk, I), lambda b, c: (b, blk(c), 0))] + state_specs
    hn, c_new, n_new, m_new = pl.pallas_call(
        functools.partial(_mlstm_kernel, n_heads=n_heads, n_chunks=nc, n_pad=n_pad,
                          scale=float(dh) ** -0.5, has_state=has_state),
        grid=(B, nc),
        in_specs=in_specs,
        out_specs=out_specs,
        out_shape=out_shape,
        scratch_shapes=[pltpu.VMEM((n_heads, dh, dh), F32), pltpu.VMEM((n_heads, 1, dh), F32),
                        pltpu.VMEM((n_heads, 1, GATE_LANES), F32)],
        compiler_params=_cparams(("parallel", "arbitrary")),
        name="mlstm",
    )(*args)
    return hn, c_new, n_new[:, :, 0, :], m_new[:, :, 0, 0]


def _post_ln_kernel(*refs, gated, alpha):
    if gated:
        hn_ref, xc_ref, z_ref, gn_ref, sk_ref, res_ref, w_ref, g_ref, b_ref, o_ref = refs
        z = z_ref[...].astype(F32)
        hcomb = hn_ref[...].astype(F32) * gn_ref[...] + sk_ref[...] * xc_ref[...].astype(F32)
        lhs = (hcomb * (z * _sigmoid(z))).astype(BF16)
    else:
        lhs_ref, res_ref, w_ref, g_ref, b_ref, o_ref = refs
        lhs = lhs_ref[...]
    mix = jnp.dot(lhs, w_ref[...], preferred_element_type=F32)
    y = alpha * res_ref[...] + mix
    o_ref[...] = _layer_norm(y, g_ref[...], b_ref[...])


def _post_ln(lhs_args, res, w, ln_g, ln_b, alpha, rows, gated):
    B = res.shape[0]
    K, D = w.shape
    tm = _row_tile(rows, 544)
    row = lambda b, i: (b, i, 0)
    const = lambda b, i: (0, 0)
    if gated:
        hn, xc, z_src, gn_g, skip = lhs_args
        nz = z_src.shape[2] // K - 1
        in_specs = [pl.BlockSpec((None, tm, K), row),
                    pl.BlockSpec((None, tm, K), row),
                    pl.BlockSpec((None, tm, K), lambda b, i: (b, i, nz)),
                    pl.BlockSpec((1, K), const),
                    pl.BlockSpec((1, K), const)]
        args = [hn, xc, z_src, gn_g.reshape(1, K), skip.reshape(1, K)]
    else:
        in_specs = [pl.BlockSpec((None, tm, K), row)]
        args = [lhs_args]
    in_specs += [pl.BlockSpec((None, tm, D), row),
                 pl.BlockSpec((K, D), const),
                 pl.BlockSpec((1, D), const),
                 pl.BlockSpec((1, D), const)]
    args += [res, w, ln_g.reshape(1, D), ln_b.reshape(1, D)]
    return pl.pallas_call(
        functools.partial(_post_ln_kernel, gated=gated, alpha=alpha),
        grid=(B, rows // tm),
        in_specs=in_specs,
        out_specs=pl.BlockSpec((None, tm, D), row),
        out_shape=jax.ShapeDtypeStruct((B, rows, D), F32),
        compiler_params=_cparams(("parallel", "parallel")),
        name="post_ln_gated" if gated else "post_ln",
    )(*args)


def _mlp_kernel(x_ref, w1_ref, w2_ref, g_ref, b_ref, o_ref, acc_ref, *, alpha, n_chunks):
    c = pl.program_id(2)
    hid = jnp.dot(x_ref[...].astype(BF16), w1_ref[...], preferred_element_type=F32)
    hid = jnp.maximum(hid, 0.0)
    part = jnp.dot((hid * hid).astype(BF16), w2_ref[...], preferred_element_type=F32)

    @pl.when(c == 0)
    def _():
        acc_ref[...] = part

    @pl.when(c > 0)
    def _():
        acc_ref[...] += part

    @pl.when(c == n_chunks - 1)
    def _():
        y = alpha * x_ref[...] + acc_ref[...]
        o_ref[...] = _layer_norm(y, g_ref[...], b_ref[...])


def _mlp_ln(x, w1, w2, ln_g, ln_b, alpha, rows):
    B, _, D = x.shape
    F = w1.shape[1]
    tm, tf = _row_tile(rows, 1088), _col_tile(F, 1024)
    nc = F // tf
    return pl.pallas_call(
        functools.partial(_mlp_kernel, alpha=alpha, n_chunks=nc),
        grid=(B, rows // tm, nc),
        in_specs=[pl.BlockSpec((None, tm, D), lambda b, i, c: (b, i, 0)),
                  pl.BlockSpec((D, tf), lambda b, i, c: (0, c)),
                  pl.BlockSpec((tf, D), lambda b, i, c: (c, 0)),
                  pl.BlockSpec((1, D), lambda b, i, c: (0, 0)),
                  pl.BlockSpec((1, D), lambda b, i, c: (0, 0))],
        out_specs=pl.BlockSpec((None, tm, D), lambda b, i, c: (b, i, 0)),
        out_shape=jax.ShapeDtypeStruct((B, rows, D), F32),
        scratch_shapes=[pltpu.VMEM((tm, D), F32)],
        compiler_params=_cparams(("parallel", "parallel", "arbitrary")),
        name="mlp_ln",
    )(x, w1, w2, ln_g.reshape(1, D), ln_b.reshape(1, D))


def _qkv_a_kernel(x_ref, wq_ref, wk_ref, wv_ref, q_ref, k_ref, v_ref, pk_ref, pv_ref, *,
                  seq, n_meta, q_scale):
    x = x_ref[...].astype(BF16)
    L = x.shape[0]
    q = jnp.dot(x, wq_ref[...], preferred_element_type=F32)
    q_ref[...] = (q * q_scale).astype(q_ref.dtype)
    for w_ref, o_ref, p_ref in ((wk_ref, k_ref, pk_ref), (wv_ref, v_ref, pv_ref)):
        y = jnp.dot(x, w_ref[...], preferred_element_type=F32)
        o_ref[...] = y.astype(o_ref.dtype)
        p_ref[0:n_meta, :] = y[L - n_meta:L, :]
        p_ref[n_meta:n_meta + seq, :] = y[0:seq, :]


def _qkv_a(x, w_qkv, seq, n_meta, q_scale):
    B, L, D = x.shape
    tn = _col_tile(D, 256)
    nj = D // tn
    xs = pl.BlockSpec((None, L, D), lambda b, j: (b, 0, 0))
    os = pl.BlockSpec((None, L, tn), lambda b, j: (b, 0, j))
    ps = pl.BlockSpec((None, n_meta + seq, tn), lambda b, j: (b, 0, j))
    return pl.pallas_call(
        functools.partial(_qkv_a_kernel, seq=seq, n_meta=n_meta, q_scale=q_scale),
        grid=(B, nj),
        in_specs=[xs,
                  pl.BlockSpec((D, tn), lambda b, j: (0, j)),
                  pl.BlockSpec((D, tn), lambda b, j: (0, nj + j)),
                  pl.BlockSpec((D, tn), lambda b, j: (0, 2 * nj + j))],
        out_specs=(os, os, os, ps, ps),
        out_shape=(jax.ShapeDtypeStruct((B, L, D), BF16),) * 3
        + (jax.ShapeDtypeStruct((B, n_meta + seq, D), F32),) * 2,
        compiler_params=_cparams(("parallel", "arbitrary")),
        name="qkv_a",
    )(x, w_qkv, w_qkv, w_qkv)


def _upper_and_ones(n):
    j = lax.broadcasted_iota(jnp.int32, (n, 2 * n), 0)
    s = lax.broadcasted_iota(jnp.int32, (n, 2 * n), 1)
    return ((j > s) | (s >= n)).astype(BF16)


def _sb_weights(zns, valid, carry, su_ones):
    S = zns[0].shape[1]
    lks = []
    for zn in zns:
        lk = jnp.minimum(zn, 0.0) - LOG2E * jnp.log(1.0 + jnp.exp2(-jnp.abs(zn)))
        lks.append(lk if valid is None else jnp.where(valid, lk, 0.0))
    ress = [jnp.dot(lk.astype(BF16), su_ones, preferred_element_type=F32) for lk in lks]
    out = []
    for zn, lk, res in zip(zns, lks, ress):
        a = jnp.exp2((lk - zn) + (carry + res[:, :S]))
        out.append(a if valid is None else jnp.where(valid, a, 0.0))
        carry = carry + res[:, S:]
    return out, carry


def _attn_kernel(q_ref, k_ref, v_ref, bias_ref, o_ref, carry_sc, acc_sc, *,
                 qb, first_block, n_blocks, n_pad, dh):
    S = SEQ_BLOCK
    lb0 = first_block + pl.program_id(2) * qb
    first = lax.broadcasted_iota(jnp.int32, (S, 2 * dh), 1) < dh
    zero = jnp.zeros((S, 2 * dh), BF16)

    def tall(x):
        return jnp.concatenate([jnp.where(first, x, zero), jnp.where(first, zero, x)], axis=0)

    q_tall = jnp.concatenate([tall(q_ref[r * S:(r + 1) * S, :]) for r in range(qb)], axis=0)
    su_ones = _upper_and_ones(S)
    carry_sc[...] = jnp.zeros_like(carry_sc)
    acc_sc[...] = jnp.zeros_like(acc_sc)

    def key_tile(j, n, r0, diag, pad):
        nr = qb - r0
        rows = slice(r0 * 2 * S, qb * 2 * S)
        start = ((j - n + n_blocks) % n_blocks) * S
        if not isinstance(j, int):
            start = pl.multiple_of(start, S)
        z_all = lax.dot_general(q_tall[rows], k_ref[pl.ds(start, n * S), :], (((1,), (1,)), ((), ())),
                                preferred_element_type=F32)
        bias = bias_ref[rows, :]
        v_all = v_ref[pl.ds(start, n * S), :]
        zs = [z_all[:, c * S:(c + 1) * S] + bias for c in reversed(range(n))]
        vts = [tall(v_all[c * S:(c + 1) * S, :]) for c in reversed(range(n))]
        valid = None
        if diag or pad:
            riota = lax.broadcasted_iota(jnp.int32, (nr * 2 * S, S), 0)
            col = lax.broadcasted_iota(jnp.int32, (nr * 2 * S, S), 1)
            if diag:
                valid = (col < (riota & (S - 1))) | (riota >= 2 * S)
            if pad:
                valid = (col >= n_pad) if valid is None else valid & (col >= n_pad)
        a_list, carry = _sb_weights(zs, valid, carry_sc[rows, :], su_ones)
        carry_sc[rows, :] = carry

        def wide(a):
            a = a.astype(BF16)
            return jnp.concatenate(
                [jnp.concatenate([a[r * 2 * S:r * 2 * S + S], a[r * 2 * S + S:(r + 1) * 2 * S]], axis=1)
                 for r in range(nr)], axis=0)

        a_wide = jnp.concatenate([wide(a) for a in a_list], axis=1)
        acc_sc[r0 * S:qb * S, :] += jnp.dot(a_wide, jnp.concatenate(vts, axis=0), preferred_element_type=F32)

    for t in range(qb):
        r0 = qb - 1 - t
        key_tile(lb0 + r0, 1, r0, diag=True, pad=first_block == 0)
    if first_block > 0:
        n = 4 if qb % 4 == 0 else 2 if qb % 2 == 0 else 1

        def full(t, _):
            key_tile(lb0 - 1 - n * t, n, 0, diag=False, pad=False)
            return 0

        lax.fori_loop(0, (lb0 - 1) // n, full, 0)
        key_tile(0, 1, 0, diag=False, pad=True)
    o_ref[...] = acc_sc[...].astype(o_ref.dtype)


def _attn(q, k, v, sb_bias, n_pad, qb, first_block, n_tiles):
    B, L, D = q.shape
    H = sb_bias.shape[0]
    dh = D // H
    assert 2 * dh == SEQ_BLOCK
    S = SEQ_BLOCK
    R = qb * S
    nb = L // S
    bias = jnp.broadcast_to((-LOG2E * sb_bias.astype(F32)).reshape(H // 2, 1, 2, 1, 1), (H // 2, qb, 2, S, S))
    bias = bias.reshape(H // 2, 2 * R, S)
    q_index = lambda b, p, i: (b, (first_block + nb - 1) % nb // qb + i, p)
    return pl.pallas_call(
        functools.partial(_attn_kernel, qb=qb, first_block=first_block, n_blocks=nb, n_pad=n_pad, dh=dh),
        grid=(B, H // 2, n_tiles),
        in_specs=[pl.BlockSpec((None, R, 2 * dh), q_index),
                  pl.BlockSpec((None, L, 2 * dh), lambda b, p, i: (b, 0, p)),
                  pl.BlockSpec((None, L, 2 * dh), lambda b, p, i: (b, 0, p)),
                  pl.BlockSpec((None, 2 * R, S), lambda b, p, i: (p, 0, 0))],
        out_specs=pl.BlockSpec((None, R, 2 * dh), lambda b, p, i: (b, i, p)),
        out_shape=jax.ShapeDtypeStruct((B, n_tiles * R, D), BF16),
        scratch_shapes=[pltpu.VMEM((2 * R, S), F32), pltpu.VMEM((R, 2 * dh), F32)],
        compiler_params=_cparams(("parallel", "parallel", "arbitrary")),
        name="sb_attn",
    )(q, k, v, bias)


def _sattn_kernel(pt_ref, qbd_ref, bias_ref, kn_ref, vn_ref, *refs, n_groups, group, n_heads, dh, n_new):
    del pt_ref
    k_refs = refs[:group]
    v_refs = refs[group:2 * group]
    o_ref, acc_ref, carry_ref = refs[2 * group:]
    g = pl.program_id(1)
    qbd = qbd_ref[...]
    bias = bias_ref[...]
    R, S = bias.shape
    su_ones = _upper_and_ones(S)

    def blocks(kts, vts, valid):
        zs = [jnp.dot(qbd, kt, preferred_element_type=F32) + bias for kt in kts]
        a_list, carry = _sb_weights(zs, valid, carry_ref[...], su_ones)
        carry_ref[...] = carry
        part = None
        for a, vt in zip(a_list, vts):
            av = lax.dot_general(a.astype(BF16), vt, (((1,), (1,)), ((), ())), preferred_element_type=F32)
            part = av if part is None else part + av
        acc_ref[...] += part

    @pl.when(g == 0)
    def _():
        acc_ref[...] = jnp.zeros_like(acc_ref)
        carry_ref[...] = jnp.zeros_like(carry_ref)
        t = lax.broadcasted_iota(jnp.int32, (R, S), 0) // n_heads
        s = lax.broadcasted_iota(jnp.int32, (R, S), 1)
        blocks([kn_ref[...]], [vn_ref[...]], s < t)

    blocks([r[...].astype(BF16) for r in k_refs], [r[...].astype(BF16) for r in v_refs], None)

    @pl.when(g == n_groups - 1)
    def _():
        acc = acc_ref[...]
        row_h = lax.broadcasted_iota(jnp.int32, acc.shape, 0) % n_heads
        col_h = lax.broadcasted_iota(jnp.int32, acc.shape, 1) // dh
        own = jnp.where(row_h == col_h, acc, 0.0)
        o_ref[...] = jnp.sum(own.reshape(n_new, n_heads, acc.shape[1]), axis=1)


def _sattn(q, k_new, v_new, cache_k, cache_v, page_table, sb_bias, q_scale):
    B, T, D = q.shape
    H = sb_bias.shape[0]
    dh = D // H
    n_pool, S = cache_k.shape[0], cache_k.shape[1]
    n_pages = page_table.shape[1]
    group = 4 if n_pages % 4 == 0 else 1
    n_groups = n_pages // group
    R = T * H
    head_of_col = jnp.arange(D, dtype=jnp.int32) // dh
    own = head_of_col[None, None, :] == jnp.arange(H, dtype=jnp.int32)[None, :, None]
    qbd = jnp.where(own[:, None], (q * q_scale)[:, :, None, :], 0.0).reshape(B, R, D).astype(BF16)
    bias = jnp.broadcast_to(jnp.tile(-LOG2E * sb_bias.astype(F32), T)[:, None], (R, S))
    pad = ((0, 0), (0, S - T), (0, 0))
    kn = jnp.swapaxes(jnp.pad(k_new, pad), 1, 2).astype(BF16)
    vn = jnp.swapaxes(jnp.pad(v_new, pad), 1, 2).astype(BF16)
    ck = jnp.transpose(cache_k, (0, 2, 3, 1)).reshape(n_pool, D, S)
    cv = jnp.transpose(cache_v, (0, 2, 3, 1)).reshape(n_pool, D, S)

    def page_spec(idx):
        return pl.BlockSpec((None, D, S),
                            lambda b, g, pt: (pt[b, n_pages - 1 - (g * group + idx)], 0, 0))

    grid_spec = pltpu.PrefetchScalarGridSpec(
        num_scalar_prefetch=1,
        grid=(B, n_groups),
        in_specs=[pl.BlockSpec((None, R, D), lambda b, g, pt: (b, 0, 0)),
                  pl.BlockSpec((R, S), lambda b, g, pt: (0, 0)),
                  pl.BlockSpec((None, D, S), lambda b, g, pt: (b, 0, 0)),
                  pl.BlockSpec((None, D, S), lambda b, g, pt: (b, 0, 0))]
        + [page_spec(idx) for idx in range(group)] * 2,
        out_specs=pl.BlockSpec((None, T, D), lambda b, g, pt: (b, 0, 0)),
        scratch_shapes=[pltpu.VMEM((R, D), F32), pltpu.VMEM((R, S), F32)],
    )
    return pl.pallas_call(
        functools.partial(_sattn_kernel, n_groups=n_groups, group=group, n_heads=H, dh=dh, n_new=T),
        grid_spec=grid_spec,
        out_shape=jax.ShapeDtypeStruct((B, T, D), F32),
        compiler_params=_cparams(("parallel", "arbitrary")),
        name="sb_attn_paged",
    )(page_table, qbd, bias, kn, vn, *([ck] * group), *([cv] * group))


def _mlstm_layer(x, n_seq, rows_out, n_pad, chunk, conv_hist, state, wts, ln, alpha):
    w_up, conv_w, conv_b, w_qkv, wg, bg, gn_g, skip, w_down, n_heads = wts
    G, R, _ = x.shape
    I = conv_w.shape[1]
    L = G * R // n_seq
    up = _proj(x, w_up, BF16)
    up_seq = up.reshape(n_seq, L, 2 * I)
    if conv_hist is None:
        conv_src, zero_rows = up_seq, (L - SEQ_BLOCK, L - SEQ_BLOCK + n_pad)
    else:
        nh = conv_hist.shape[1]
        conv_src = jnp.concatenate([up_seq[:, :n_pad - nh, :I], conv_hist.astype(BF16),
                                    up_seq[:, n_pad:, :I]], axis=1)
        zero_rows = None
    xc = _conv_silu(conv_src, conv_w, conv_b, zero_rows).reshape(G, R, I)
    qkv = _qkv_m(xc, up, w_qkv)
    gates = _gates(qkv, wg, bg, n_heads)
    hn, c_new, n_new, m_new = _mlstm(qkv.reshape(n_seq, L, 3 * I), gates.reshape(n_seq, L, GATE_LANES),
                                     n_heads, n_pad, chunk, state)
    x1 = _post_ln((hn.reshape(G, R, I), xc, up, gn_g, skip), x, w_down, ln[0], ln[1], alpha, rows_out,
                  gated=True)
    return x1, up_seq, (c_new, n_new, m_new)


def kernel(x_prompt, x_sample, state_C, state_n, state_m, state_conv, cache_k, cache_v, page_table,
           meta_tokens, w_up, conv_w, conv_b, w_q, w_k, w_v, w_gate, b_gate, gn_g, skip, w_down,
           w_qkv, w_o, sb_bias, mlp_w1, mlp_w2, ln_g, ln_b):
    B, SEQ, D = x_prompt.shape
    BD, T, _ = x_sample.shape
    depth = mlp_w1.shape[0]
    n_meta = meta_tokens.shape[0]
    I = conv_w.shape[2]
    width = conv_w.shape[1]
    HM = b_gate.shape[1] // 2
    HA = sb_bias.shape[1]
    alpha = (2.0 * depth) ** 0.25
    q_scale = -LOG2E * float(D // HA) ** -0.5
    S = SEQ_BLOCK
    assert SEQ % S == 0 and n_meta <= S and T + width - 1 <= SAMPLE_ROWS
    qb = ATTN_Q_BLOCKS
    while (SEQ // S) % qb:
        qb //= 2

    n_pad_p = S - n_meta
    LP = SEQ + S
    xp = jnp.concatenate([x_prompt, jnp.zeros((B, n_pad_p, D), F32),
                          jnp.broadcast_to(meta_tokens.astype(F32)[None], (B, n_meta, D))], axis=1)
    n_pad_s = SAMPLE_ROWS - T
    RS = BD * SAMPLE_ROWS
    xs = jnp.pad(x_sample, ((0, 0), (n_pad_s, 0), (0, 0))).reshape(1, RS, D)

    outs = {name: [] for name in ("p_C", "p_n", "p_m", "p_conv", "p_k", "p_v",
                                  "s_C", "s_n", "s_m", "s_conv", "s_k", "s_v")}
    for i in range(depth):
        j = i // 2
        last = i == depth - 1
        rows_p = SEQ if last else LP
        lng, lnb = ln_g[i], ln_b[i]
        w1, w2 = mlp_w1[i].astype(BF16), mlp_w2[i].astype(BF16)
        if i % 2 == 0:
            wg = jnp.pad(w_gate[j], ((0, 0), (0, GATE_LANES - 2 * HM))).astype(BF16)
            bg = jnp.pad(b_gate[j], (0, GATE_LANES - 2 * HM)).reshape(1, GATE_LANES).astype(F32)
            wts = (w_up[j].astype(BF16), conv_w[j], conv_b[j],
                   jnp.concatenate([w_q[j], w_k[j], w_v[j]], axis=1).astype(BF16), wg, bg,
                   gn_g[j], skip[j], w_down[j].astype(BF16), HM)
            xp1, up_p, (Cp, np_, mp) = _mlstm_layer(xp, B, rows_p, n_pad_p, S, None, None, wts,
                                                    (lng[0], lnb[0]), alpha)
            xs1, up_s, (Cs, ns, ms) = _mlstm_layer(xs, BD, RS, n_pad_s, SAMPLE_ROWS, state_conv[j],
                                                   (state_C[j], state_n[j], state_m[j]), wts,
                                                   (lng[0], lnb[0]), alpha)
            outs["p_C"].append(Cp); outs["p_n"].append(np_); outs["p_m"].append(mp)
            outs["p_conv"].append(up_p[:, SEQ - (width - 1):SEQ, :I].astype(F32))
            outs["s_C"].append(Cs); outs["s_n"].append(ns); outs["s_m"].append(ms)
            hist = jnp.concatenate([state_conv[j], up_s[:, n_pad_s:, :I].astype(F32)], axis=1)
            outs["s_conv"].append(hist[:, T:])
        else:
            wqkv = w_qkv[j].astype(BF16)
            wo = w_o[j].astype(BF16)
            q, k, v, pk, pv = _qkv_a(xp, wqkv, SEQ, n_meta, q_scale)
            o = _attn(q, k, v, sb_bias[j], n_pad_p, qb, 1, SEQ // (qb * S))
            if not last:
                o_meta = _attn(q, k, v, sb_bias[j], n_pad_p, 1, 0, 1)
                o = jnp.concatenate([o, o_meta], axis=1)
            xp1 = _post_ln(o, xp, wo, lng[0], lnb[0], alpha, rows_p, gated=False)
            outs["p_k"].append(pk.reshape(B, n_meta + SEQ, HA, D // HA))
            outs["p_v"].append(pv.reshape(B, n_meta + SEQ, HA, D // HA))
            qkv_s = _proj(xs, wqkv, F32).reshape(BD, SAMPLE_ROWS, 3 * D)[:, n_pad_s:]
            qs, ks, vs = qkv_s[..., :D], qkv_s[..., D:2 * D], qkv_s[..., 2 * D:]
            o_s = _sattn(qs, ks, vs, cache_k[j], cache_v[j], page_table, sb_bias[j], q_scale)
            o_s = jnp.pad(o_s, ((0, 0), (n_pad_s, 0), (0, 0))).astype(BF16).reshape(1, RS, D)
            xs1 = _post_ln(o_s, xs, wo, lng[0], lnb[0], alpha, RS, gated=False)
            outs["s_k"].append(ks.reshape(BD, T, HA, D // HA))
            outs["s_v"].append(vs.reshape(BD, T, HA, D // HA))
        xp = _mlp_ln(xp1, w1, w2, lng[1], lnb[1], alpha, rows_p)
        xs = _mlp_ln(xs1, w1, w2, lng[1], lnb[1], alpha, RS)
    y_prompt = xp[:, :SEQ]
    y_sample = xs.reshape(BD, SAMPLE_ROWS, D)[:, n_pad_s:]
    st = lambda name: jnp.stack(outs[name])
    return (y_prompt, y_sample, st("p_C"), st("p_n"), st("p_m"), st("p_conv"), st("p_k"), st("p_v"),
            st("s_C"), st("s_n"), st("s_m"), st("s_conv"), st("s_k"), st("s_v"))
```

```python
import functools

import jax
import jax.numpy as jnp
from jax import lax
from jax.experimental import pallas as pl
from jax.experimental.pallas import tpu as pltpu

F32 = jnp.float32
BF16 = jnp.bfloat16

LN_EPS = 1e-5
GN_EPS = 1e-6
NEG = -1e30
LOG2E = 1.4426950408889634
SEQ_BLOCK = 128
SAMPLE_ROWS = 16
GATE_LANES = 128
ATTN_Q_BLOCKS = 8
VMEM_LIMIT = 56 * 1024 * 1024


def _cparams(sem):
    return pltpu.CompilerParams(dimension_semantics=sem, vmem_limit_bytes=VMEM_LIMIT)


def _row_tile(n, cap):
    best = None
    for t in range(16, min(n, cap) + 1, 16):
        if n % t == 0:
            best = t
    assert best is not None, (n, cap)
    return best


def _col_tile(n, cap):
    best = None
    for t in range(128, min(n, cap) + 1, 128):
        if n % t == 0:
            best = t
    assert best is not None, (n, cap)
    return best


def _sigmoid(x):
    return 1.0 / (1.0 + jnp.exp(-x))


def _softplus(x):
    return jnp.maximum(x, 0.0) + jnp.log1p(jnp.exp(-jnp.abs(x)))


def _layer_norm(y, g, b):
    mu = jnp.mean(y, axis=-1, keepdims=True)
    yc = y - mu
    var = jnp.mean(yc * yc, axis=-1, keepdims=True)
    return yc * lax.rsqrt(var + LN_EPS) * g + b


def _proj_kernel(x_ref, w_ref, o_ref):
    o_ref[...] = jnp.dot(x_ref[...].astype(BF16), w_ref[...],
                         preferred_element_type=F32).astype(o_ref.dtype)


def _proj(x, w, out_dtype, tm_cap=1088, tn_cap=1024):
    B, L, K = x.shape
    N = w.shape[1]
    tm, tn = _row_tile(L, tm_cap), _col_tile(N, tn_cap)
    return pl.pallas_call(
        _proj_kernel,
        grid=(B, L // tm, N // tn),
        in_specs=[pl.BlockSpec((None, tm, K), lambda b, i, j: (b, i, 0)),
                  pl.BlockSpec((K, tn), lambda b, i, j: (0, j))],
        out_specs=pl.BlockSpec((None, tm, tn), lambda b, i, j: (b, i, j)),
        out_shape=jax.ShapeDtypeStruct((B, L, N), out_dtype),
        compiler_params=_cparams(("parallel", "parallel", "arbitrary")),
        name="proj",
    )(x, w)


def _qkv_m_kernel(xc_ref, xm_ref, w_ref, o_ref, *, n_qk):
    j = pl.program_id(2)

    @pl.when(j < n_qk)
    def _():
        o_ref[...] = jnp.dot(xc_ref[...], w_ref[...], preferred_element_type=F32).astype(o_ref.dtype)

    @pl.when(j >= n_qk)
    def _():
        o_ref[...] = jnp.dot(xm_ref[...], w_ref[...], preferred_element_type=F32).astype(o_ref.dtype)


def _qkv_m(xc, xm_src, w_qkv):
    B, L, I = xc.shape
    tm, tn = _row_tile(L, 1088), _col_tile(I, 1024)
    return pl.pallas_call(
        functools.partial(_qkv_m_kernel, n_qk=2 * I // tn),
        grid=(B, L // tm, 3 * I // tn),
        in_specs=[pl.BlockSpec((None, tm, I), lambda b, i, j: (b, i, 0)),
                  pl.BlockSpec((None, tm, I), lambda b, i, j: (b, i, 0)),
                  pl.BlockSpec((I, tn), lambda b, i, j: (0, j))],
        out_specs=pl.BlockSpec((None, tm, tn), lambda b, i, j: (b, i, j)),
        out_shape=jax.ShapeDtypeStruct((B, L, 3 * I), BF16),
        compiler_params=_cparams(("parallel", "parallel", "arbitrary")),
        name="qkv_m",
    )(xc, xm_src, w_qkv)


def _gates_kernel(x_ref, w_ref, b_ref, o_ref, *, n_heads):
    g = jnp.dot(x_ref[...], w_ref[...], preferred_element_type=F32) + b_ref[...]
    lane = lax.broadcasted_iota(jnp.int32, g.shape, 1)
    o_ref[...] = jnp.where(lane < n_heads, g, -_softplus(-g))


def _gates(qkv, wg, bg, n_heads):
    B, L, K = qkv.shape
    tm = _row_tile(L, 544)
    return pl.pallas_call(
        functools.partial(_gates_kernel, n_heads=n_heads),
        grid=(B, L // tm),
        in_specs=[pl.BlockSpec((None, tm, K), lambda b, i: (b, i, 0)),
                  pl.BlockSpec((K, GATE_LANES), lambda b, i: (0, 0)),
                  pl.BlockSpec((1, GATE_LANES), lambda b, i: (0, 0))],
        out_specs=pl.BlockSpec((None, tm, GATE_LANES), lambda b, i: (b, i, 0)),
        out_shape=jax.ShapeDtypeStruct((B, L, GATE_LANES), F32),
        compiler_params=_cparams(("parallel", "parallel")),
        name="gates",
    )(qkv, wg, bg)


def _conv_kernel(x_ref, w_ref, b_ref, o_ref, *, width, zero_rows):
    x = x_ref[...].astype(F32)
    if zero_rows is not None:
        row = lax.broadcasted_iota(jnp.int32, x.shape, 0)
        x = jnp.where((row >= zero_rows[0]) & (row < zero_rows[1]), 0.0, x)
    acc = b_ref[...] + w_ref[width - 1:width, :] * x
    for j in range(width - 1):
        acc = acc + w_ref[j:j + 1, :] * pltpu.roll(x, width - 1 - j, axis=0)
    o_ref[...] = (acc * _sigmoid(acc)).astype(o_ref.dtype)


def _conv_silu(x_src, conv_w, conv_b, zero_rows):
    B, L, _ = x_src.shape
    width, I = conv_w.shape
    tc = _col_tile(I, 256)
    return pl.pallas_call(
        functools.partial(_conv_kernel, width=width, zero_rows=zero_rows),
        grid=(B, I // tc),
        in_specs=[pl.BlockSpec((None, L, tc), lambda b, j: (b, 0, j)),
                  pl.BlockSpec((width, tc), lambda b, j: (0, j)),
                  pl.BlockSpec((1, tc), lambda b, j: (0, j))],
        out_specs=pl.BlockSpec((None, L, tc), lambda b, j: (b, 0, j)),
        out_shape=jax.ShapeDtypeStruct((B, L, I), BF16),
        compiler_params=_cparams(("parallel", "parallel")),
        name="conv_silu",
    )(x_src, conv_w, conv_b.reshape(1, I))


def _mlstm_kernel(*refs, n_heads, n_chunks, n_pad, scale, has_state):
    if has_state:
        (q_ref, k_ref, v_ref, g_ref, gt_ref, c0_ref, n0_ref, m0_ref,
         h_ref, c_out, n_out, m_out, c_sc, n_sc, m_sc) = refs
    else:
        (q_ref, k_ref, v_ref, g_ref, gt_ref,
         h_ref, c_out, n_out, m_out, c_sc, n_sc, m_sc) = refs
    c = pl.program_id(1)
    L = q_ref.shape[0]
    dh = q_ref.shape[1] // n_heads

    @pl.when(c == 0)
    def _():
        if has_state:
            c_sc[...] = c0_ref[...]
            n_sc[...] = n0_ref[...]
            m_sc[...] = m0_ref[...]
        else:
            c_sc[...] = jnp.zeros_like(c_sc)
            n_sc[...] = jnp.zeros_like(n_sc)
            m_sc[...] = jnp.zeros_like(m_sc)

    n_pad_c = jnp.where(c == 0, n_pad, 0)
    g = g_ref[...]
    row_g = lax.broadcasted_iota(jnp.int32, g.shape, 0)
    lane_g = lax.broadcasted_iota(jnp.int32, g.shape, 1)
    g = jnp.where(row_g < n_pad_c, jnp.where(lane_g < n_heads, NEG, 0.0), g)
    gt = gt_ref[...]
    sub_t = lax.broadcasted_iota(jnp.int32, gt.shape, 0)
    lane_t = lax.broadcasted_iota(jnp.int32, gt.shape, 1)
    gt = jnp.where(lane_t < n_pad_c, jnp.where(sub_t < n_heads, NEG, 0.0), gt)

    r = lax.broadcasted_iota(jnp.int32, (L, L), 0)
    s = lax.broadcasted_iota(jnp.int32, (L, L), 1)
    causal = s <= r
    tri = causal.astype(F32)
    cum_col = jnp.dot(tri, jnp.where(lane_g >= n_heads, g, 0.0),
                      preferred_element_type=F32, precision=lax.Precision.HIGHEST)
    cum_row = lax.dot_general(jnp.where(sub_t >= n_heads, gt, 0.0), tri, (((1,), (1,)), ((), ())),
                              preferred_element_type=F32, precision=lax.Precision.HIGHEST)

    for hd in range(n_heads):
        cols = slice(hd * dh, (hd + 1) * dh)
        b_col = cum_col[:, n_heads + hd:n_heads + hd + 1]
        ig_col = g[:, hd:hd + 1]
        b_row = cum_row[n_heads + hd:n_heads + hd + 1, :]
        ig_row = gt[hd:hd + 1, :]
        b_end = jnp.sum(gt[n_heads + hd:n_heads + hd + 1, :], axis=1, keepdims=True)
        m = m_sc[hd][:, 0:1]
        q = q_ref[:, cols]
        k = k_ref[:, cols]
        v = v_ref[:, cols]

        d = jnp.where(causal, b_col - b_row + ig_row, NEG)
        m_inter = b_col + m
        m_t = jnp.maximum(m_inter, jnp.max(d, axis=1, keepdims=True))
        qk = lax.dot_general(q, k, (((1,), (1,)), ((), ())), preferred_element_type=F32)
        sw = qk * (scale * jnp.exp(d - m_t))
        sc = jnp.exp(m_inter - m_t)
        cb = c_sc[hd].astype(BF16)
        num = (jnp.dot(sw.astype(BF16), v, preferred_element_type=F32)
               + sc * jnp.dot(q, cb, preferred_element_type=F32))
        den = (jnp.sum(sw, axis=1, keepdims=True)
               + sc * jnp.sum(q.astype(F32) * n_sc[hd], axis=1, keepdims=True))
        hh = num * (1.0 / jnp.maximum(jnp.abs(den), jnp.exp(-m_t)))
        mu = jnp.mean(hh, axis=1, keepdims=True)
        hc = hh - mu
        var = jnp.mean(hc * hc, axis=1, keepdims=True)
        h_ref[:, cols] = (hc * lax.rsqrt(var + GN_EPS)).astype(h_ref.dtype)

        g_col = b_end - b_col + ig_col
        g_row = b_end - b_row + ig_row
        m_new = jnp.maximum(b_end + m, jnp.max(g_row, axis=1, keepdims=True))
        w_col = scale * jnp.exp(g_col - m_new)
        decay = jnp.exp(b_end + m - m_new)
        vw = (v.astype(F32) * w_col).astype(BF16)
        c_sc[hd] = decay * c_sc[hd] + lax.dot_general(k, vw, (((0,), (0,)), ((), ())),
                                                      preferred_element_type=F32)
        n_sc[hd] = decay * n_sc[hd] + jnp.sum(k.astype(F32) * w_col, axis=0, keepdims=True)
        m_sc[hd] = jnp.broadcast_to(m_new, m_sc.shape[1:])

    @pl.when(c == n_chunks - 1)
    def _():
        c_out[...] = c_sc[...]
        n_out[...] = n_sc[...]
        m_out[...] = m_sc[...]


def _mlstm(qkv, gates, n_heads, n_pad, chunk, state=None):
    B, L, I3 = qkv.shape
    I = I3 // 3
    dh = I // n_heads
    nc = L // chunk
    assert 2 * n_heads <= 8
    gates_t = jnp.swapaxes(gates[:, :, :8], 1, 2)
    has_state = state is not None

    def blk(c):
        return (c + nc - 1) % nc

    state_specs = [pl.BlockSpec((None, n_heads, dh, dh), lambda b, c: (b, 0, 0, 0)),
                   pl.BlockSpec((None, n_heads, 1, dh), lambda b, c: (b, 0, 0, 0)),
                   pl.BlockSpec((None, n_heads, 1, GATE_LANES), lambda b, c: (b, 0, 0, 0))]
    in_specs = [pl.BlockSpec((None, chunk, I), lambda b, c: (b, blk(c), 0)),
                pl.BlockSpec((None, chunk, I), lambda b, c: (b, blk(c), 1)),
                pl.BlockSpec((None, chunk, I), lambda b, c: (b, blk(c), 2)),
                pl.BlockSpec((None, chunk, GATE_LANES), lambda b, c: (b, blk(c), 0)),
                pl.BlockSpec((None, 8, chunk), lambda b, c: (b, 0, blk(c)))]
    args = [qkv, qkv, qkv, gates, gates_t]
    if has_state:
        c0, n0, m0 = state
        in_specs += state_specs
        args += [c0, n0.reshape(B, n_heads, 1, dh),
                 jnp.broadcast_to(m0.reshape(B, n_heads, 1, 1), (B, n_heads, 1, GATE_LANES))]
    out_shape = (jax.ShapeDtypeStruct((B, L, I), BF16),
                 jax.ShapeDtypeStruct((B, n_heads, dh, dh), F32),
                 jax.ShapeDtypeStruct((B, n_heads, 1, dh), F32),
                 jax.ShapeDtypeStruct((B, n_heads, 1, GATE_LANES), F32))
    out_specs = [pl.BlockSpec((None, chunk, I), lambda b, c: (b, blk(c), 0))] + state_specs
    hn, c_new, n_new, m_new = pl.pallas_call(
        functools.partial(_mlstm_kernel, n_heads=n_heads, n_chunks=nc, n_pad=n_pad,
                          scale=float(dh) ** -0.5, has_state=has_state),
        grid=(B, nc),
        in_specs=in_specs,
        out_specs=out_specs,
        out_shape=out_shape,
        scratch_shapes=[pltpu.VMEM((n_heads, dh, dh), F32), pltpu.VMEM((n_heads, 1, dh), F32),
                        pltpu.VMEM((n_heads, 1, GATE_LANES), F32)],
        compiler_params=_cparams(("parallel", "arbitrary")),
        name="mlstm",
    )(*args)
    return hn, c_new, n_new[:, :, 0, :], m_new[:, :, 0, 0]


def _post_ln_kernel(*refs, gated, alpha):
    if gated:
        hn_ref, xc_ref, z_ref, gn_ref, sk_ref, res_ref, w_ref, g_ref, b_ref, o_ref = refs
        z = z_ref[...].astype(F32)
        hcomb = hn_ref[...].astype(F32) * gn_ref[...] + sk_ref[...] * xc_ref[...].astype(F32)
        lhs = (hcomb * (z * _sigmoid(z))).astype(BF16)
    else:
        lhs_ref, res_ref, w_ref, g_ref, b_ref, o_ref = refs
        lhs = lhs_ref[...]
    mix = jnp.dot(lhs, w_ref[...], preferred_element_type=F32)
    y = alpha * res_ref[...] + mix
    o_ref[...] = _layer_norm(y, g_ref[...], b_ref[...])


def _post_ln(lhs_args, res, w, ln_g, ln_b, alpha, rows, gated):
    B = res.shape[0]
    K, D = w.shape
    tm = _row_tile(rows, 544)
    row = lambda b, i: (b, i, 0)
    const = lambda b, i: (0, 0)
    if gated:
        hn, xc, z_src, gn_g, skip = lhs_args
        nz = z_src.shape[2] // K - 1
        in_specs = [pl.BlockSpec((None, tm, K), row),
                    pl.BlockSpec((None, tm, K), row),
                    pl.BlockSpec((None, tm, K), lambda b, i: (b, i, nz)),
                    pl.BlockSpec((1, K), const),
                    pl.BlockSpec((1, K), const)]
        args = [hn, xc, z_src, gn_g.reshape(1, K), skip.reshape(1, K)]
    else:
        in_specs = [pl.BlockSpec((None, tm, K), row)]
        args = [lhs_args]
    in_specs += [pl.BlockSpec((None, tm, D), row),
                 pl.BlockSpec((K, D), const),
                 pl.BlockSpec((1, D), const),
                 pl.BlockSpec((1, D), const)]
    args += [res, w, ln_g.reshape(1, D), ln_b.reshape(1, D)]
    return pl.pallas_call(
        functools.partial(_post_ln_kernel, gated=gated, alpha=alpha),
        grid=(B, rows // tm),
        in_specs=in_specs,
        out_specs=pl.BlockSpec((None, tm, D), row),
        out_shape=jax.ShapeDtypeStruct((B, rows, D), F32),
        compiler_params=_cparams(("parallel", "parallel")),
        name="post_ln_gated" if gated else "post_ln",
    )(*args)


def _mlp_kernel(x_ref, w1_ref, w2_ref, g_ref, b_ref, o_ref, acc_ref, *, alpha, n_chunks):
    c = pl.program_id(2)
    hid = jnp.dot(x_ref[...].astype(BF16), w1_ref[...], preferred_element_type=F32)
    hid = jnp.maximum(hid, 0.0)
    part = jnp.dot((hid * hid).astype(BF16), w2_ref[...], preferred_element_type=F32)

    @pl.when(c == 0)
    def _():
        acc_ref[...] = part

    @pl.when(c > 0)
    def _():
        acc_ref[...] += part

    @pl.when(c == n_chunks - 1)
    def _():
        y = alpha * x_ref[...] + acc_ref[...]
        o_ref[...] = _layer_norm(y, g_ref[...], b_ref[...])


def _mlp_ln(x, w1, w2, ln_g, ln_b, alpha, rows):
    B, _, D = x.shape
    F = w1.shape[1]
    tm, tf = _row_tile(rows, 1088), _col_tile(F, 1024)
    nc = F // tf
    return pl.pallas_call(
        functools.partial(_mlp_kernel, alpha=alpha, n_chunks=nc),
        grid=(B, rows // tm, nc),
        in_specs=[pl.BlockSpec((None, tm, D), lambda b, i, c: (b, i, 0)),
                  pl.BlockSpec((D, tf), lambda b, i, c: (0, c)),
                  pl.BlockSpec((tf, D), lambda b, i, c: (c, 0)),
                  pl.BlockSpec((1, D), lambda b, i, c: (0, 0)),
                  pl.BlockSpec((1, D), lambda b, i, c: (0, 0))],
        out_specs=pl.BlockSpec((None, tm, D), lambda b, i, c: (b, i, 0)),
        out_shape=jax.ShapeDtypeStruct((B, rows, D), F32),
        scratch_shapes=[pltpu.VMEM((tm, D), F32)],
        compiler_params=_cparams(("parallel", "parallel", "arbitrary")),
        name="mlp_ln",
    )(x, w1, w2, ln_g.reshape(1, D), ln_b.reshape(1, D))


def _qkv_a_kernel(x_ref, wq_ref, wk_ref, wv_ref, q_ref, k_ref, v_ref, pk_ref, pv_ref, *,
                  seq, n_meta, q_scale):
    x = x_ref[...].astype(BF16)
    L = x.shape[0]
    q = jnp.dot(x, wq_ref[...], preferred_element_type=F32)
    q_ref[...] = (q * q_scale).astype(q_ref.dtype)
    for w_ref, o_ref, p_ref in ((wk_ref, k_ref, pk_ref), (wv_ref, v_ref, pv_ref)):
        y = jnp.dot(x, w_ref[...], preferred_element_type=F32)
        o_ref[...] = y.astype(o_ref.dtype)
        p_ref[0:n_meta, :] = y[L - n_meta:L, :]
        p_ref[n_meta:n_meta + seq, :] = y[0:seq, :]


def _qkv_a(x, w_qkv, seq, n_meta, q_scale):
    B, L, D = x.shape
    tn = _col_tile(D, 256)
    nj = D // tn
    xs = pl.BlockSpec((None, L, D), lambda b, j: (b, 0, 0))
    os = pl.BlockSpec((None, L, tn), lambda b, j: (b, 0, j))
    ps = pl.BlockSpec((None, n_meta + seq, tn), lambda b, j: (b, 0, j))
    return pl.pallas_call(
        functools.partial(_qkv_a_kernel, seq=seq, n_meta=n_meta, q_scale=q_scale),
        grid=(B, nj),
        in_specs=[xs,
                  pl.BlockSpec((D, tn), lambda b, j: (0, j)),
                  pl.BlockSpec((D, tn), lambda b, j: (0, nj + j)),
                  pl.BlockSpec((D, tn), lambda b, j: (0, 2 * nj + j))],
        out_specs=(os, os, os, ps, ps),
        out_shape=(jax.ShapeDtypeStruct((B, L, D), BF16),) * 3
        + (jax.ShapeDtypeStruct((B, n_meta + seq, D), F32),) * 2,
        compiler_params=_cparams(("parallel", "arbitrary")),
        name="qkv_a",
    )(x, w_qkv, w_qkv, w_qkv)


def _upper_and_ones(n):
    j = lax.broadcasted_iota(jnp.int32, (n, 2 * n), 0)
    s = lax.broadcasted_iota(jnp.int32, (n, 2 * n), 1)
    return ((j > s) | (s >= n)).astype(BF16)


def _sb_weights(zns, valid, carry, su_ones):
    S = zns[0].shape[1]
    valid = [None] * len(zns) if valid is None else valid
    lks = []
    for zn, ok in zip(zns, valid):
        lk = jnp.minimum(zn, 0.0) - LOG2E * jnp.log(1.0 + jnp.exp2(-jnp.abs(zn)))
        lks.append(lk if ok is None else jnp.where(ok, lk, 0.0))
    ress = [jnp.dot(lk.astype(BF16), su_ones, preferred_element_type=F32) for lk in lks]
    out = []
    for zn, lk, res, ok in zip(zns, lks, ress, valid):
        a = jnp.exp2((lk - zn) + (carry + res[:, :S]))
        out.append(a if ok is None else jnp.where(ok, a, 0.0))
        carry = carry + res[:, S:]
    return out, carry


def _attn_kernel(q_ref, k_ref, v_ref, bias_ref, o_ref, carry_sc, acc_sc, *,
                 qb, first_block, n_blocks, n_pad, dh):
    S = SEQ_BLOCK
    lb0 = first_block + pl.program_id(2) * qb
    first = lax.broadcasted_iota(jnp.int32, (S, 2 * dh), 1) < dh
    zero = jnp.zeros((S, 2 * dh), BF16)

    def tall(x):
        return jnp.concatenate([jnp.where(first, x, zero), jnp.where(first, zero, x)], axis=0)

    q_tall = jnp.concatenate([tall(q_ref[r * S:(r + 1) * S, :]) for r in range(qb)], axis=0)
    su_ones = _upper_and_ones(S)
    carry_sc[...] = jnp.zeros_like(carry_sc)
    acc_sc[...] = jnp.zeros_like(acc_sc)

    def key_tile(j, n, r0, diag, pad):
        nr = qb - r0
        rows = slice(r0 * 2 * S, qb * 2 * S)
        start = ((j - n + n_blocks) % n_blocks) * S
        if not isinstance(j, int):
            start = pl.multiple_of(start, S)
        z_all = lax.dot_general(q_tall[rows], k_ref[pl.ds(start, n * S), :], (((1,), (1,)), ((), ())),
                                preferred_element_type=F32)
        bias = bias_ref[rows, :]
        v_all = v_ref[pl.ds(start, n * S), :]
        zs = [z_all[:, c * S:(c + 1) * S] + bias for c in reversed(range(n))]
        vts = [tall(v_all[c * S:(c + 1) * S, :]) for c in reversed(range(n))]
        valid = None
        if diag or pad:
            riota = lax.broadcasted_iota(jnp.int32, (nr * 2 * S, S), 0)
            col = lax.broadcasted_iota(jnp.int32, (nr * 2 * S, S), 1)
            valid = [None] * n
            if diag:
                inside = col < (riota & (S - 1))
                valid = [(riota >= (c + 1) * 2 * S) | ((riota >= c * 2 * S) & inside) for c in reversed(range(n))]
            if pad:
                valid = [(col >= n_pad) if v is None else v & (col >= n_pad) for v in valid]
        a_list, carry = _sb_weights(zs, valid, carry_sc[rows, :], su_ones)
        carry_sc[rows, :] = carry

        def wide(a):
            a = a.astype(BF16)
            return jnp.concatenate(
                [jnp.concatenate([a[r * 2 * S:r * 2 * S + S], a[r * 2 * S + S:(r + 1) * 2 * S]], axis=1)
                 for r in range(nr)], axis=0)

        a_wide = jnp.concatenate([wide(a) for a in a_list], axis=1)
        acc_sc[r0 * S:qb * S, :] += jnp.dot(a_wide, jnp.concatenate(vts, axis=0), preferred_element_type=F32)

    nd = 2 if qb % 2 == 0 else 1
    for t in range(qb // nd):
        r0 = qb - nd * (t + 1)
        key_tile(lb0 + r0 + nd - 1, nd, r0, diag=True, pad=first_block == 0)
    if first_block > 0:
        n = 4 if qb % 4 == 0 else 2 if qb % 2 == 0 else 1

        def full(t, _):
            key_tile(lb0 - 1 - n * t, n, 0, diag=False, pad=False)
            return 0

        lax.fori_loop(0, (lb0 - 1) // n, full, 0)
        key_tile(0, 1, 0, diag=False, pad=True)
    o_ref[...] = acc_sc[...].astype(o_ref.dtype)


def _attn(q, k, v, sb_bias, n_pad, qb, first_block, n_tiles):
    B, L, D = q.shape
    H = sb_bias.shape[0]
    dh = D // H
    assert 2 * dh == SEQ_BLOCK
    S = SEQ_BLOCK
    R = qb * S
    nb = L // S
    bias = jnp.broadcast_to((-LOG2E * sb_bias.astype(F32)).reshape(H // 2, 1, 2, 1, 1), (H // 2, qb, 2, S, S))
    bias = bias.reshape(H // 2, 2 * R, S)
    q_index = lambda b, p, i: (b, (first_block + nb - 1) % nb // qb + i, p)
    return pl.pallas_call(
        functools.partial(_attn_kernel, qb=qb, first_block=first_block, n_blocks=nb, n_pad=n_pad, dh=dh),
        grid=(B, H // 2, n_tiles),
        in_specs=[pl.BlockSpec((None, R, 2 * dh), q_index),
                  pl.BlockSpec((None, L, 2 * dh), lambda b, p, i: (b, 0, p)),
                  pl.BlockSpec((None, L, 2 * dh), lambda b, p, i: (b, 0, p)),
                  pl.BlockSpec((None, 2 * R, S), lambda b, p, i: (p, 0, 0))],
        out_specs=pl.BlockSpec((None, R, 2 * dh), lambda b, p, i: (b, i, p)),
        out_shape=jax.ShapeDtypeStruct((B, n_tiles * R, D), BF16),
        scratch_shapes=[pltpu.VMEM((2 * R, S), F32), pltpu.VMEM((R, 2 * dh), F32)],
        compiler_params=_cparams(("parallel", "parallel", "arbitrary")),
        name="sb_attn",
    )(q, k, v, bias)


def _sattn_kernel(pt_ref, qbd_ref, bias_ref, kn_ref, vn_ref, *refs, n_groups, group, n_heads, dh, n_new):
    del pt_ref
    k_refs = refs[:group]
    v_refs = refs[group:2 * group]
    o_ref, acc_ref, carry_ref = refs[2 * group:]
    g = pl.program_id(1)
    qbd = qbd_ref[...]
    bias = bias_ref[...]
    R, S = bias.shape
    su_ones = _upper_and_ones(S)

    def blocks(kts, vts, valid):
        zs = [jnp.dot(qbd, kt, preferred_element_type=F32) + bias for kt in kts]
        a_list, carry = _sb_weights(zs, valid, carry_ref[...], su_ones)
        carry_ref[...] = carry
        part = None
        for a, vt in zip(a_list, vts):
            av = lax.dot_general(a.astype(BF16), vt, (((1,), (1,)), ((), ())), preferred_element_type=F32)
            part = av if part is None else part + av
        acc_ref[...] += part

    @pl.when(g == 0)
    def _():
        acc_ref[...] = jnp.zeros_like(acc_ref)
        carry_ref[...] = jnp.zeros_like(carry_ref)
        t = lax.broadcasted_iota(jnp.int32, (R, S), 0) // n_heads
        s = lax.broadcasted_iota(jnp.int32, (R, S), 1)
        blocks([kn_ref[...]], [vn_ref[...]], [s < t])

    blocks([r[...].astype(BF16) for r in k_refs], [r[...].astype(BF16) for r in v_refs], None)

    @pl.when(g == n_groups - 1)
    def _():
        acc = acc_ref[...]
        row_h = lax.broadcasted_iota(jnp.int32, acc.shape, 0) % n_heads
        col_h = lax.broadcasted_iota(jnp.int32, acc.shape, 1) // dh
        own = jnp.where(row_h == col_h, acc, 0.0)
        o_ref[...] = jnp.sum(own.reshape(n_new, n_heads, acc.shape[1]), axis=1)


def _sattn(q, k_new, v_new, cache_k, cache_v, page_table, sb_bias, q_scale):
    B, T, D = q.shape
    H = sb_bias.shape[0]
    dh = D // H
    n_pool, S = cache_k.shape[0], cache_k.shape[1]
    n_pages = page_table.shape[1]
    group = max(g for g in (8, 4, 2, 1) if n_pages % g == 0)
    n_groups = n_pages // group
    R = T * H
    head_of_col = jnp.arange(D, dtype=jnp.int32) // dh
    own = head_of_col[None, None, :] == jnp.arange(H, dtype=jnp.int32)[None, :, None]
    qbd = jnp.where(own[:, None], (q * q_scale)[:, :, None, :], 0.0).reshape(B, R, D).astype(BF16)
    bias = jnp.broadcast_to(jnp.tile(-LOG2E * sb_bias.astype(F32), T)[:, None], (R, S))
    pad = ((0, 0), (0, S - T), (0, 0))
    kn = jnp.swapaxes(jnp.pad(k_new, pad), 1, 2).astype(BF16)
    vn = jnp.swapaxes(jnp.pad(v_new, pad), 1, 2).astype(BF16)
    ck = jnp.transpose(cache_k, (0, 2, 3, 1)).reshape(n_pool, D, S)
    cv = jnp.transpose(cache_v, (0, 2, 3, 1)).reshape(n_pool, D, S)

    def page_spec(idx):
        return pl.BlockSpec((None, D, S),
                            lambda b, g, pt: (pt[b, n_pages - 1 - (g * group + idx)], 0, 0))

    grid_spec = pltpu.PrefetchScalarGridSpec(
        num_scalar_prefetch=1,
        grid=(B, n_groups),
        in_specs=[pl.BlockSpec((None, R, D), lambda b, g, pt: (b, 0, 0)),
                  pl.BlockSpec((R, S), lambda b, g, pt: (0, 0)),
                  pl.BlockSpec((None, D, S), lambda b, g, pt: (b, 0, 0)),
                  pl.BlockSpec((None, D, S), lambda b, g, pt: (b, 0, 0))]
        + [page_spec(idx) for idx in range(group)] * 2,
        out_specs=pl.BlockSpec((None, T, D), lambda b, g, pt: (b, 0, 0)),
        scratch_shapes=[pltpu.VMEM((R, D), F32), pltpu.VMEM((R, S), F32)],
    )
    return pl.pallas_call(
        functools.partial(_sattn_kernel, n_groups=n_groups, group=group, n_heads=H, dh=dh, n_new=T),
        grid_spec=grid_spec,
        out_shape=jax.ShapeDtypeStruct((B, T, D), F32),
        compiler_params=_cparams(("parallel", "arbitrary")),
        name="sb_attn_paged",
    )(page_table, qbd, bias, kn, vn, *([ck] * group), *([cv] * group))


def _mlstm_layer(x, n_seq, rows_out, n_pad, chunk, conv_hist, state, wts, ln, alpha):
    w_up, conv_w, conv_b, w_qkv, wg, bg, gn_g, skip, w_down, n_heads = wts
    G, R, _ = x.shape
    I = conv_w.shape[1]
    L = G * R // n_seq
    up = _proj(x, w_up, BF16)
    up_seq = up.reshape(n_seq, L, 2 * I)
    if conv_hist is None:
        conv_src, zero_rows = up_seq, (L - SEQ_BLOCK, L - SEQ_BLOCK + n_pad)
    else:
        nh = conv_hist.shape[1]
        conv_src = jnp.concatenate([up_seq[:, :n_pad - nh, :I], conv_hist.astype(BF16),
                                    up_seq[:, n_pad:, :I]], axis=1)
        zero_rows = None
    xc = _conv_silu(conv_src.reshape(G, R, conv_src.shape[2]), conv_w, conv_b, zero_rows)
    qkv = _qkv_m(xc, up, w_qkv)
    gates = _gates(qkv, wg, bg, n_heads)
    hn, c_new, n_new, m_new = _mlstm(qkv.reshape(n_seq, L, 3 * I), gates.reshape(n_seq, L, GATE_LANES),
                                     n_heads, n_pad, chunk, state)
    x1 = _post_ln((hn.reshape(G, R, I), xc, up, gn_g, skip), x, w_down, ln[0], ln[1], alpha, rows_out,
                  gated=True)
    return x1, up_seq, (c_new, n_new, m_new)


def kernel(x_prompt, x_sample, state_C, state_n, state_m, state_conv, cache_k, cache_v, page_table,
           meta_tokens, w_up, conv_w, conv_b, w_q, w_k, w_v, w_gate, b_gate, gn_g, skip, w_down,
           w_qkv, w_o, sb_bias, mlp_w1, mlp_w2, ln_g, ln_b):
    B, SEQ, D = x_prompt.shape
    BD, T, _ = x_sample.shape
    depth = mlp_w1.shape[0]
    n_meta = meta_tokens.shape[0]
    I = conv_w.shape[2]
    width = conv_w.shape[1]
    HM = b_gate.shape[1] // 2
    HA = sb_bias.shape[1]
    alpha = (2.0 * depth) ** 0.25
    q_scale = -LOG2E * float(D // HA) ** -0.5
    S = SEQ_BLOCK
    assert SEQ % S == 0 and n_meta <= S and T + width - 1 <= SAMPLE_ROWS
    qb = ATTN_Q_BLOCKS
    while (SEQ // S) % qb:
        qb //= 2

    n_pad_p = S - n_meta
    LP = SEQ + S
    xp = jnp.concatenate([x_prompt, jnp.zeros((B, n_pad_p, D), F32),
                          jnp.broadcast_to(meta_tokens.astype(F32)[None], (B, n_meta, D))], axis=1)
    n_pad_s = SAMPLE_ROWS - T
    RS = BD * SAMPLE_ROWS
    xs = jnp.pad(x_sample, ((0, 0), (n_pad_s, 0), (0, 0))).reshape(1, RS, D)

    outs = {name: [] for name in ("p_C", "p_n", "p_m", "p_conv", "p_k", "p_v",
                                  "s_C", "s_n", "s_m", "s_conv", "s_k", "s_v")}
    for i in range(depth):
        j = i // 2
        last = i == depth - 1
        rows_p = SEQ if last else LP
        lng, lnb = ln_g[i], ln_b[i]
        w1, w2 = mlp_w1[i].astype(BF16), mlp_w2[i].astype(BF16)
        if i % 2 == 0:
            wg = jnp.pad(w_gate[j], ((0, 0), (0, GATE_LANES - 2 * HM))).astype(BF16)
            bg = jnp.pad(b_gate[j], (0, GATE_LANES - 2 * HM)).reshape(1, GATE_LANES).astype(F32)
            wts = (w_up[j].astype(BF16), conv_w[j], conv_b[j],
                   jnp.concatenate([w_q[j], w_k[j], w_v[j]], axis=1).astype(BF16), wg, bg,
                   gn_g[j], skip[j], w_down[j].astype(BF16), HM)
            xp1, up_p, (Cp, np_, mp) = _mlstm_layer(xp, B, rows_p, n_pad_p, S, None, None, wts,
                                                    (lng[0], lnb[0]), alpha)
            xs1, up_s, (Cs, ns, ms) = _mlstm_layer(xs, BD, RS, n_pad_s, SAMPLE_ROWS, state_conv[j],
                                                   (state_C[j], state_n[j], state_m[j]), wts,
                                                   (lng[0], lnb[0]), alpha)
            outs["p_C"].append(Cp); outs["p_n"].append(np_); outs["p_m"].append(mp)
            outs["p_conv"].append(up_p[:, SEQ - (width - 1):SEQ, :I].astype(F32))
            outs["s_C"].append(Cs); outs["s_n"].append(ns); outs["s_m"].append(ms)
            hist = jnp.concatenate([state_conv[j], up_s[:, n_pad_s:, :I].astype(F32)], axis=1)
            outs["s_conv"].append(hist[:, T:])
        else:
            wqkv = w_qkv[j].astype(BF16)
            wo = w_o[j].astype(BF16)
            q, k, v, pk, pv = _qkv_a(xp, wqkv, SEQ, n_meta, q_scale)
            o = _attn(q, k, v, sb_bias[j], n_pad_p, qb, 1, SEQ // (qb * S))
            if not last:
                o_meta = _attn(q, k, v, sb_bias[j], n_pad_p, 1, 0, 1)
                o = jnp.concatenate([o, o_meta], axis=1)
            xp1 = _post_ln(o, xp, wo, lng[0], lnb[0], alpha, rows_p, gated=False)
            outs["p_k"].append(pk.reshape(B, n_meta + SEQ, HA, D // HA))
            outs["p_v"].append(pv.reshape(B, n_meta + SEQ, HA, D // HA))
            qkv_s = _proj(xs, wqkv, F32).reshape(BD, SAMPLE_ROWS, 3 * D)[:, n_pad_s:]
            qs, ks, vs = qkv_s[..., :D], qkv_s[..., D:2 * D], qkv_s[..., 2 * D:]
            o_s = _sattn(qs, ks, vs, cache_k[j], cache_v[j], page_table, sb_bias[j], q_scale)
            o_s = jnp.pad(o_s, ((0, 0), (n_pad_s, 0), (0, 0))).astype(BF16).reshape(1, RS, D)
            xs1 = _post_ln(o_s, xs, wo, lng[0], lnb[0], alpha, RS, gated=False)
            outs["s_k"].append(ks.reshape(BD, T, HA, D // HA))
            outs["s_v"].append(vs.reshape(BD, T, HA, D // HA))
        xp = _mlp_ln(xp1, w1, w2, lng[1], lnb[1], alpha, rows_p)
        xs = _mlp_ln(xs1, w1, w2, lng[1], lnb[1], alpha, RS)
    y_prompt = xp[:, :SEQ]
    y_sample = xs.reshape(BD, SAMPLE_ROWS, D)[:, n_pad_s:]
    st = lambda name: jnp.stack(outs[name])
    return (y_prompt, y_sample, st("p_C"), st("p_n"), st("p_m"), st("p_conv"), st("p_k"), st("p_v"),
            st("s_C"), st("s_n"), st("s_m"), st("s_conv"), st("s_k"), st("s_v"))
```

```python
import functools

import jax
import jax.numpy as jnp
from jax import lax
from jax.experimental import pallas as pl
from jax.experimental.pallas import tpu as pltpu

F32 = jnp.float32
BF16 = jnp.bfloat16

LN_EPS = 1e-5
GN_EPS = 1e-6
NEG = -1e30
LOG2E = 1.4426950408889634
SEQ_BLOCK = 128
SAMPLE_ROWS = 16
GATE_LANES = 128
ATTN_Q_BLOCKS = 8
VMEM_LIMIT = 56 * 1024 * 1024


def _cparams(sem):
    return pltpu.CompilerParams(dimension_semantics=sem, vmem_limit_bytes=VMEM_LIMIT)


def _row_tile(n, cap):
    best = None
    for t in range(16, min(n, cap) + 1, 16):
        if n % t == 0:
            best = t
    assert best is not None, (n, cap)
    return best


def _col_tile(n, cap):
    best = None
    for t in range(128, min(n, cap) + 1, 128):
        if n % t == 0:
            best = t
    assert best is not None, (n, cap)
    return best


def _sigmoid(x):
    return 1.0 / (1.0 + jnp.exp(-x))


def _softplus(x):
    return jnp.maximum(x, 0.0) + jnp.log1p(jnp.exp(-jnp.abs(x)))


def _layer_norm(y, g, b):
    mu = jnp.mean(y, axis=-1, keepdims=True)
    yc = y - mu
    var = jnp.mean(yc * yc, axis=-1, keepdims=True)
    return yc * lax.rsqrt(var + LN_EPS) * g + b


def _proj_kernel(x_ref, w_ref, o_ref):
    o_ref[...] = jnp.dot(x_ref[...].astype(BF16), w_ref[...],
                         preferred_element_type=F32).astype(o_ref.dtype)


def _proj(x, w, out_dtype, tm_cap=1088, tn_cap=1024):
    B, L, K = x.shape
    N = w.shape[1]
    tm, tn = _row_tile(L, tm_cap), _col_tile(N, tn_cap)
    return pl.pallas_call(
        _proj_kernel,
        grid=(B, L // tm, N // tn),
        in_specs=[pl.BlockSpec((None, tm, K), lambda b, i, j: (b, i, 0)),
                  pl.BlockSpec((K, tn), lambda b, i, j: (0, j))],
        out_specs=pl.BlockSpec((None, tm, tn), lambda b, i, j: (b, i, j)),
        out_shape=jax.ShapeDtypeStruct((B, L, N), out_dtype),
        compiler_params=_cparams(("parallel", "parallel", "arbitrary")),
        name="proj",
    )(x, w)


def _qkv_m_kernel(xc_ref, xm_ref, w_ref, o_ref, *, n_qk):
    j = pl.program_id(2)

    @pl.when(j < n_qk)
    def _():
        o_ref[...] = jnp.dot(xc_ref[...], w_ref[...], preferred_element_type=F32).astype(o_ref.dtype)

    @pl.when(j >= n_qk)
    def _():
        o_ref[...] = jnp.dot(xm_ref[...], w_ref[...], preferred_element_type=F32).astype(o_ref.dtype)


def _qkv_m(xc, xm_src, w_qkv):
    B, L, I = xc.shape
    tm, tn = _row_tile(L, 1088), _col_tile(I, 1024)
    return pl.pallas_call(
        functools.partial(_qkv_m_kernel, n_qk=2 * I // tn),
        grid=(B, L // tm, 3 * I // tn),
        in_specs=[pl.BlockSpec((None, tm, I), lambda b, i, j: (b, i, 0)),
                  pl.BlockSpec((None, tm, I), lambda b, i, j: (b, i, 0)),
                  pl.BlockSpec((I, tn), lambda b, i, j: (0, j))],
        out_specs=pl.BlockSpec((None, tm, tn), lambda b, i, j: (b, i, j)),
        out_shape=jax.ShapeDtypeStruct((B, L, 3 * I), BF16),
        compiler_params=_cparams(("parallel", "parallel", "arbitrary")),
        name="qkv_m",
    )(xc, xm_src, w_qkv)


def _gates_kernel(x_ref, w_ref, b_ref, o_ref, *, n_heads):
    g = jnp.dot(x_ref[...], w_ref[...], preferred_element_type=F32) + b_ref[...]
    lane = lax.broadcasted_iota(jnp.int32, g.shape, 1)
    o_ref[...] = jnp.where(lane < n_heads, g, -_softplus(-g))


def _gates(qkv, wg, bg, n_heads):
    B, L, K = qkv.shape
    tm = _row_tile(L, 544)
    return pl.pallas_call(
        functools.partial(_gates_kernel, n_heads=n_heads),
        grid=(B, L // tm),
        in_specs=[pl.BlockSpec((None, tm, K), lambda b, i: (b, i, 0)),
                  pl.BlockSpec((K, GATE_LANES), lambda b, i: (0, 0)),
                  pl.BlockSpec((1, GATE_LANES), lambda b, i: (0, 0))],
        out_specs=pl.BlockSpec((None, tm, GATE_LANES), lambda b, i: (b, i, 0)),
        out_shape=jax.ShapeDtypeStruct((B, L, GATE_LANES), F32),
        compiler_params=_cparams(("parallel", "parallel")),
        name="gates",
    )(qkv, wg, bg)


def _conv_kernel(x_ref, w_ref, b_ref, o_ref, *, width, zero_rows):
    x = x_ref[...].astype(F32)
    if zero_rows is not None:
        row = lax.broadcasted_iota(jnp.int32, x.shape, 0)
        x = jnp.where((row >= zero_rows[0]) & (row < zero_rows[1]), 0.0, x)
    acc = b_ref[...] + w_ref[width - 1:width, :] * x
    for j in range(width - 1):
        acc = acc + w_ref[j:j + 1, :] * pltpu.roll(x, width - 1 - j, axis=0)
    o_ref[...] = (acc * _sigmoid(acc)).astype(o_ref.dtype)


def _conv_silu(x_src, conv_w, conv_b, zero_rows):
    B, L, _ = x_src.shape
    width, I = conv_w.shape
    tc = _col_tile(I, 256)
    return pl.pallas_call(
        functools.partial(_conv_kernel, width=width, zero_rows=zero_rows),
        grid=(B, I // tc),
        in_specs=[pl.BlockSpec((None, L, tc), lambda b, j: (b, 0, j)),
                  pl.BlockSpec((width, tc), lambda b, j: (0, j)),
                  pl.BlockSpec((1, tc), lambda b, j: (0, j))],
        out_specs=pl.BlockSpec((None, L, tc), lambda b, j: (b, 0, j)),
        out_shape=jax.ShapeDtypeStruct((B, L, I), BF16),
        compiler_params=_cparams(("parallel", "parallel")),
        name="conv_silu",
    )(x_src, conv_w, conv_b.reshape(1, I))


def _mlstm_kernel(*refs, n_heads, n_chunks, n_pad, scale, has_state):
    if has_state:
        (q_ref, k_ref, v_ref, g_ref, gt_ref, c0_ref, n0_ref, m0_ref,
         h_ref, c_out, n_out, m_out, c_sc, n_sc, m_sc) = refs
    else:
        (q_ref, k_ref, v_ref, g_ref, gt_ref,
         h_ref, c_out, n_out, m_out, c_sc, n_sc, m_sc) = refs
    c = pl.program_id(1)
    L = q_ref.shape[0]
    dh = q_ref.shape[1] // n_heads

    @pl.when(c == 0)
    def _():
        if has_state:
            c_sc[...] = c0_ref[...]
            n_sc[...] = n0_ref[...]
            m_sc[...] = m0_ref[...]
        else:
            c_sc[...] = jnp.zeros_like(c_sc)
            n_sc[...] = jnp.zeros_like(n_sc)
            m_sc[...] = jnp.zeros_like(m_sc)

    n_pad_c = jnp.where(c == 0, n_pad, 0)
    g = g_ref[...]
    row_g = lax.broadcasted_iota(jnp.int32, g.shape, 0)
    lane_g = lax.broadcasted_iota(jnp.int32, g.shape, 1)
    g = jnp.where(row_g < n_pad_c, jnp.where(lane_g < n_heads, NEG, 0.0), g)
    gt = gt_ref[...]
    sub_t = lax.broadcasted_iota(jnp.int32, gt.shape, 0)
    lane_t = lax.broadcasted_iota(jnp.int32, gt.shape, 1)
    gt = jnp.where(lane_t < n_pad_c, jnp.where(sub_t < n_heads, NEG, 0.0), gt)

    r = lax.broadcasted_iota(jnp.int32, (L, L), 0)
    s = lax.broadcasted_iota(jnp.int32, (L, L), 1)
    causal = s <= r
    tri = causal.astype(F32)
    cum_col = jnp.dot(tri, jnp.where(lane_g >= n_heads, g, 0.0),
                      preferred_element_type=F32, precision=lax.Precision.HIGHEST)
    cum_row = lax.dot_general(jnp.where(sub_t >= n_heads, gt, 0.0), tri, (((1,), (1,)), ((), ())),
                              preferred_element_type=F32, precision=lax.Precision.HIGHEST)

    heads = range(n_heads)
    cols = [slice(hd * dh, (hd + 1) * dh) for hd in heads]
    b_col = [cum_col[:, n_heads + hd:n_heads + hd + 1] for hd in heads]
    ig_col = [g[:, hd:hd + 1] for hd in heads]
    b_row = [cum_row[n_heads + hd:n_heads + hd + 1, :] for hd in heads]
    ig_row = [gt[hd:hd + 1, :] for hd in heads]
    b_end = [jnp.sum(gt[n_heads + hd:n_heads + hd + 1, :], axis=1, keepdims=True) for hd in heads]
    m = [m_sc[hd][:, 0:1] for hd in heads]
    q = [q_ref[:, cs] for cs in cols]
    k = [k_ref[:, cs] for cs in cols]
    v = [v_ref[:, cs] for cs in cols]

    nt = (((1,), (1,)), ((), ()))
    qk = [lax.dot_general(q[h], k[h], nt, preferred_element_type=F32) for h in heads]
    qc = [jnp.dot(q[h], c_sc[h].astype(BF16), preferred_element_type=F32) for h in heads]
    d = [jnp.where(causal, b_col[h] - b_row[h] + ig_row[h], NEG) for h in heads]
    m_inter = [b_col[h] + m[h] for h in heads]
    m_t = [jnp.maximum(m_inter[h], jnp.max(d[h], axis=1, keepdims=True)) for h in heads]
    sw = [qk[h] * (scale * jnp.exp(d[h] - m_t[h])) for h in heads]
    sc = [jnp.exp(m_inter[h] - m_t[h]) for h in heads]
    num = [jnp.dot(sw[h].astype(BF16), v[h], preferred_element_type=F32) + sc[h] * qc[h] for h in heads]
    den = [jnp.sum(sw[h], axis=1, keepdims=True)
           + sc[h] * jnp.sum(q[h].astype(F32) * n_sc[h], axis=1, keepdims=True) for h in heads]
    hh = [num[h] * (1.0 / jnp.maximum(jnp.abs(den[h]), jnp.exp(-m_t[h]))) for h in heads]
    mu = [jnp.mean(hh[h], axis=1, keepdims=True) for h in heads]
    hc = [hh[h] - mu[h] for h in heads]
    var = [jnp.mean(hc[h] * hc[h], axis=1, keepdims=True) for h in heads]
    for h in heads:
        h_ref[:, cols[h]] = (hc[h] * lax.rsqrt(var[h] + GN_EPS)).astype(h_ref.dtype)

    g_col = [b_end[h] - b_col[h] + ig_col[h] for h in heads]
    g_row = [b_end[h] - b_row[h] + ig_row[h] for h in heads]
    m_new = [jnp.maximum(b_end[h] + m[h], jnp.max(g_row[h], axis=1, keepdims=True)) for h in heads]
    w_col = [scale * jnp.exp(g_col[h] - m_new[h]) for h in heads]
    decay = [jnp.exp(b_end[h] + m[h] - m_new[h]) for h in heads]
    vw = [(v[h].astype(F32) * w_col[h]).astype(BF16) for h in heads]
    tn = (((0,), (0,)), ((), ()))
    kv = [lax.dot_general(k[h], vw[h], tn, preferred_element_type=F32) for h in heads]
    for h in heads:
        c_sc[h] = decay[h] * c_sc[h] + kv[h]
        n_sc[h] = decay[h] * n_sc[h] + jnp.sum(k[h].astype(F32) * w_col[h], axis=0, keepdims=True)
        m_sc[h] = jnp.broadcast_to(m_new[h], m_sc.shape[1:])

    @pl.when(c == n_chunks - 1)
    def _():
        c_out[...] = c_sc[...]
        n_out[...] = n_sc[...]
        m_out[...] = m_sc[...]


def _mlstm(qkv, gates, n_heads, n_pad, chunk, state=None):
    B, L, I3 = qkv.shape
    I = I3 // 3
    dh = I // n_heads
    nc = L // chunk
    assert 2 * n_heads <= 8
    gates_t = jnp.swapaxes(gates[:, :, :8], 1, 2)
    has_state = state is not None

    def blk(c):
        return (c + nc - 1) % nc

    state_specs = [pl.BlockSpec((None, n_heads, dh, dh), lambda b, c: (b, 0, 0, 0)),
                   pl.BlockSpec((None, n_heads, 1, dh), lambda b, c: (b, 0, 0, 0)),
                   pl.BlockSpec((None, n_heads, 1, GATE_LANES), lambda b, c: (b, 0, 0, 0))]
    in_specs = [pl.BlockSpec((None, chunk, I), lambda b, c: (b, blk(c), 0)),
                pl.BlockSpec((None, chunk, I), lambda b, c: (b, blk(c), 1)),
                pl.BlockSpec((None, chunk, I), lambda b, c: (b, blk(c), 2)),
                pl.BlockSpec((None, chunk, GATE_LANES), lambda b, c: (b, blk(c), 0)),
                pl.BlockSpec((None, 8, chunk), lambda b, c: (b, 0, blk(c)))]
    args = [qkv, qkv, qkv, gates, gates_t]
    if has_state:
        c0, n0, m0 = state
        in_specs += state_specs
        args += [c0, n0.reshape(B, n_heads, 1, dh),
                 jnp.broadcast_to(m0.reshape(B, n_heads, 1, 1), (B, n_heads, 1, GATE_LANES))]
    out_shape = (jax.ShapeDtypeStruct((B, L, I), BF16),
                 jax.ShapeDtypeStruct((B, n_heads, dh, dh), F32),
                 jax.ShapeDtypeStruct((B, n_heads, 1, dh), F32),
                 jax.ShapeDtypeStruct((B, n_heads, 1, GATE_LANES), F32))
    out_specs = [pl.BlockSpec((None, chunk, I), lambda b, c: (b, blk(c), 0))] + state_specs
    hn, c_new, n_new, m_new = pl.pallas_call(
        functools.partial(_mlstm_kernel, n_heads=n_heads, n_chunks=nc, n_pad=n_pad,
                          scale=float(dh) ** -0.5, has_state=has_state),
        grid=(B, nc),
        in_specs=in_specs,
        out_specs=out_specs,
        out_shape=out_shape,
        scratch_shapes=[pltpu.VMEM((n_heads, dh, dh), F32), pltpu.VMEM((n_heads, 1, dh), F32),
                        pltpu.VMEM((n_heads, 1, GATE_LANES), F32)],
        compiler_params=_cparams(("parallel", "arbitrary")),
        name="mlstm",
    )(*args)
    return hn, c_new, n_new[:, :, 0, :], m_new[:, :, 0, 0]


def _post_ln_kernel(*refs, gated, alpha):
    if gated:
        hn_ref, xc_ref, z_ref, gn_ref, sk_ref, res_ref, w_ref, g_ref, b_ref, o_ref = refs
        z = z_ref[...].astype(F32)
        hcomb = hn_ref[...].astype(F32) * gn_ref[...] + sk_ref[...] * xc_ref[...].astype(F32)
        lhs = (hcomb * (z * _sigmoid(z))).astype(BF16)
    else:
        lhs_ref, res_ref, w_ref, g_ref, b_ref, o_ref = refs
        lhs = lhs_ref[...]
    mix = jnp.dot(lhs, w_ref[...], preferred_element_type=F32)
    y = alpha * res_ref[...] + mix
    o_ref[...] = _layer_norm(y, g_ref[...], b_ref[...])


def _post_ln(lhs_args, res, w, ln_g, ln_b, alpha, rows, gated):
    B = res.shape[0]
    K, D = w.shape
    tm = _row_tile(rows, 544)
    row = lambda b, i: (b, i, 0)
    const = lambda b, i: (0, 0)
    if gated:
        hn, xc, z_src, gn_g, skip = lhs_args
        nz = z_src.shape[2] // K - 1
        in_specs = [pl.BlockSpec((None, tm, K), row),
                    pl.BlockSpec((None, tm, K), row),
                    pl.BlockSpec((None, tm, K), lambda b, i: (b, i, nz)),
                    pl.BlockSpec((1, K), const),
                    pl.BlockSpec((1, K), const)]
        args = [hn, xc, z_src, gn_g.reshape(1, K), skip.reshape(1, K)]
    else:
        in_specs = [pl.BlockSpec((None, tm, K), row)]
        args = [lhs_args]
    in_specs += [pl.BlockSpec((None, tm, D), row),
                 pl.BlockSpec((K, D), const),
                 pl.BlockSpec((1, D), const),
                 pl.BlockSpec((1, D), const)]
    args += [res, w, ln_g.reshape(1, D), ln_b.reshape(1, D)]
    return pl.pallas_call(
        functools.partial(_post_ln_kernel, gated=gated, alpha=alpha),
        grid=(B, rows // tm),
        in_specs=in_specs,
        out_specs=pl.BlockSpec((None, tm, D), row),
        out_shape=jax.ShapeDtypeStruct((B, rows, D), F32),
        compiler_params=_cparams(("parallel", "parallel")),
        name="post_ln_gated" if gated else "post_ln",
    )(*args)


def _mlp_kernel(x_ref, w1_ref, w2_ref, g_ref, b_ref, o_ref, *, alpha, n_chunks):
    x = x_ref[...]
    xb = x.astype(BF16)
    tf = w1_ref.shape[1] // n_chunks
    mix = None
    for c in range(n_chunks):
        hid = jnp.maximum(jnp.dot(xb, w1_ref[:, c * tf:(c + 1) * tf], preferred_element_type=F32), 0.0)
        part = jnp.dot((hid * hid).astype(BF16), w2_ref[c * tf:(c + 1) * tf, :], preferred_element_type=F32)
        mix = part if mix is None else mix + part
    o_ref[...] = _layer_norm(alpha * x + mix, g_ref[...], b_ref[...])


def _mlp_ln(x, w1, w2, ln_g, ln_b, alpha, rows):
    B, _, D = x.shape
    F = w1.shape[1]
    tm = _row_tile(rows, 544)
    once = pl.Buffered(1)
    return pl.pallas_call(
        functools.partial(_mlp_kernel, alpha=alpha, n_chunks=F // _col_tile(F, 2048)),
        grid=(B, rows // tm),
        in_specs=[pl.BlockSpec((None, tm, D), lambda b, i: (b, i, 0)),
                  pl.BlockSpec((D, F), lambda b, i: (0, 0), pipeline_mode=once),
                  pl.BlockSpec((F, D), lambda b, i: (0, 0), pipeline_mode=once),
                  pl.BlockSpec((1, D), lambda b, i: (0, 0)),
                  pl.BlockSpec((1, D), lambda b, i: (0, 0))],
        out_specs=pl.BlockSpec((None, tm, D), lambda b, i: (b, i, 0)),
        out_shape=jax.ShapeDtypeStruct((B, rows, D), F32),
        compiler_params=_cparams(("parallel", "parallel")),
        name="mlp_ln",
    )(x, w1, w2, ln_g.reshape(1, D), ln_b.reshape(1, D))


def _qkv_a_kernel(x_ref, wq_ref, wk_ref, wv_ref, q_ref, k_ref, v_ref, pk_ref, pv_ref, *,
                  seq, n_meta, q_scale):
    x = x_ref[...].astype(BF16)
    L = x.shape[0]
    q = jnp.dot(x, wq_ref[...], preferred_element_type=F32)
    q_ref[...] = (q * q_scale).astype(q_ref.dtype)
    for w_ref, o_ref, p_ref in ((wk_ref, k_ref, pk_ref), (wv_ref, v_ref, pv_ref)):
        y = jnp.dot(x, w_ref[...], preferred_element_type=F32)
        o_ref[...] = y.astype(o_ref.dtype)
        p_ref[0:n_meta, :] = y[L - n_meta:L, :]
        p_ref[n_meta:n_meta + seq, :] = y[0:seq, :]


def _qkv_a(x, w_qkv, seq, n_meta, q_scale):
    B, L, D = x.shape
    tn = _col_tile(D, 256)
    nj = D // tn
    xs = pl.BlockSpec((None, L, D), lambda b, j: (b, 0, 0))
    os = pl.BlockSpec((None, L, tn), lambda b, j: (b, 0, j))
    ps = pl.BlockSpec((None, n_meta + seq, tn), lambda b, j: (b, 0, j))
    return pl.pallas_call(
        functools.partial(_qkv_a_kernel, seq=seq, n_meta=n_meta, q_scale=q_scale),
        grid=(B, nj),
        in_specs=[xs,
                  pl.BlockSpec((D, tn), lambda b, j: (0, j)),
                  pl.BlockSpec((D, tn), lambda b, j: (0, nj + j)),
                  pl.BlockSpec((D, tn), lambda b, j: (0, 2 * nj + j))],
        out_specs=(os, os, os, ps, ps),
        out_shape=(jax.ShapeDtypeStruct((B, L, D), BF16),) * 3
        + (jax.ShapeDtypeStruct((B, n_meta + seq, D), F32),) * 2,
        compiler_params=_cparams(("parallel", "arbitrary")),
        name="qkv_a",
    )(x, w_qkv, w_qkv, w_qkv)


def _upper_and_ones(n):
    j = lax.broadcasted_iota(jnp.int32, (n, 2 * n), 0)
    s = lax.broadcasted_iota(jnp.int32, (n, 2 * n), 1)
    return ((j > s) | (s >= n)).astype(BF16)


def _strict_upper(n):
    j = lax.broadcasted_iota(jnp.int32, (n, n), 0)
    s = lax.broadcasted_iota(jnp.int32, (n, n), 1)
    return (j > s).astype(BF16)


def _sb_weights(zns, valid, carry, su_ones, su_pair):
    S = zns[0].shape[1]
    valid = [None] * len(zns) if valid is None else valid
    lks = []
    for zn, ok in zip(zns, valid):
        lk = jnp.minimum(zn, 0.0) - LOG2E * jnp.log(1.0 + jnp.exp2(-jnp.abs(zn)))
        lks.append(lk if ok is None else jnp.where(ok, lk, 0.0))
    n_pairs = len(zns) // 2
    afters, totals = [], []
    for p in range(n_pairs):
        late, early = lks[2 * p], lks[2 * p + 1]
        res = jnp.dot(jnp.concatenate([early, late], axis=1).astype(BF16), su_pair, preferred_element_type=F32)
        afters += [res[:, S:], res[:, :S]]
        totals += [None, res[:, 0:1] + early[:, 0:1]]
    for lk in lks[2 * n_pairs:]:
        res = jnp.dot(lk.astype(BF16), su_ones, preferred_element_type=F32)
        afters.append(res[:, :S])
        totals.append(res[:, S:])
    out = []
    for zn, lk, after, total, ok in zip(zns, lks, afters, totals, valid):
        a = jnp.exp2((lk - zn) + (carry + after))
        out.append(a if ok is None else jnp.where(ok, a, 0.0))
        if total is not None:
            carry = carry + total
    return out, carry


def _attn_kernel(q_ref, k_ref, v_ref, bias_ref, o_ref, carry_sc, acc_sc, *,
                 qb, first_block, n_blocks, n_pad, dh):
    S = SEQ_BLOCK
    lb0 = first_block + pl.program_id(2) * qb
    first = lax.broadcasted_iota(jnp.int32, (S, 2 * dh), 1) < dh
    zero = jnp.zeros((S, 2 * dh), BF16)

    def tall(x):
        return jnp.concatenate([jnp.where(first, x, zero), jnp.where(first, zero, x)], axis=0)

    q_tall = jnp.concatenate([tall(q_ref[r * S:(r + 1) * S, :]) for r in range(qb)], axis=0)
    su_ones, su_pair = _upper_and_ones(S), _strict_upper(2 * S)
    carry_sc[...] = jnp.zeros_like(carry_sc)
    acc_sc[...] = jnp.zeros_like(acc_sc)

    def key_tile(j, n, r0, diag, pad):
        nr = qb - r0
        rows = slice(r0 * 2 * S, qb * 2 * S)
        start = ((j - n + n_blocks) % n_blocks) * S
        if not isinstance(j, int):
            start = pl.multiple_of(start, S)
        z_all = lax.dot_general(q_tall[rows], k_ref[pl.ds(start, n * S), :], (((1,), (1,)), ((), ())),
                                preferred_element_type=F32)
        bias = bias_ref[rows, :]
        v_all = v_ref[pl.ds(start, n * S), :]
        zs = [z_all[:, c * S:(c + 1) * S] + bias for c in reversed(range(n))]
        vts = [tall(v_all[c * S:(c + 1) * S, :]) for c in reversed(range(n))]
        valid = None
        if diag or pad:
            riota = lax.broadcasted_iota(jnp.int32, (nr * 2 * S, S), 0)
            col = lax.broadcasted_iota(jnp.int32, (nr * 2 * S, S), 1)
            valid = [None] * n
            if diag:
                inside = col < (riota & (S - 1))
                valid = [(riota >= (c + 1) * 2 * S) | ((riota >= c * 2 * S) & inside) for c in reversed(range(n))]
            if pad:
                valid = [(col >= n_pad) if v is None else v & (col >= n_pad) for v in valid]
        a_list, carry = _sb_weights(zs, valid, carry_sc[rows, :], su_ones, su_pair)
        carry_sc[rows, :] = carry

        def wide(a):
            a = a.astype(BF16)
            return jnp.concatenate(
                [jnp.concatenate([a[r * 2 * S:r * 2 * S + S], a[r * 2 * S + S:(r + 1) * 2 * S]], axis=1)
                 for r in range(nr)], axis=0)

        a_wide = jnp.concatenate([wide(a) for a in a_list], axis=1)
        acc_sc[r0 * S:qb * S, :] += jnp.dot(a_wide, jnp.concatenate(vts, axis=0), preferred_element_type=F32)

    nd = 2 if qb % 2 == 0 else 1
    for t in range(qb // nd):
        r0 = qb - nd * (t + 1)
        key_tile(lb0 + r0 + nd - 1, nd, r0, diag=True, pad=first_block == 0)
    if first_block > 0:
        n = 4 if qb % 4 == 0 else 2 if qb % 2 == 0 else 1

        def full(t, _):
            key_tile(lb0 - 1 - n * t, n, 0, diag=False, pad=False)
            return 0

        lax.fori_loop(0, (lb0 - 1) // n, full, 0)
        key_tile(0, 1, 0, diag=False, pad=True)
    o_ref[...] = acc_sc[...].astype(o_ref.dtype)


def _attn(q, k, v, sb_bias, n_pad, qb, first_block, n_tiles):
    B, L, D = q.shape
    H = sb_bias.shape[0]
    dh = D // H
    assert 2 * dh == SEQ_BLOCK
    S = SEQ_BLOCK
    R = qb * S
    nb = L // S
    bias = jnp.broadcast_to((-LOG2E * sb_bias.astype(F32)).reshape(H // 2, 1, 2, 1, 1), (H // 2, qb, 2, S, S))
    bias = bias.reshape(H // 2, 2 * R, S)
    q_index = lambda b, p, i: (b, (first_block + nb - 1) % nb // qb + i, p)
    return pl.pallas_call(
        functools.partial(_attn_kernel, qb=qb, first_block=first_block, n_blocks=nb, n_pad=n_pad, dh=dh),
        grid=(B, H // 2, n_tiles),
        in_specs=[pl.BlockSpec((None, R, 2 * dh), q_index),
                  pl.BlockSpec((None, L, 2 * dh), lambda b, p, i: (b, 0, p)),
                  pl.BlockSpec((None, L, 2 * dh), lambda b, p, i: (b, 0, p)),
                  pl.BlockSpec((None, 2 * R, S), lambda b, p, i: (p, 0, 0))],
        out_specs=pl.BlockSpec((None, R, 2 * dh), lambda b, p, i: (b, i, p)),
        out_shape=jax.ShapeDtypeStruct((B, n_tiles * R, D), BF16),
        scratch_shapes=[pltpu.VMEM((2 * R, S), F32), pltpu.VMEM((R, 2 * dh), F32)],
        compiler_params=_cparams(("parallel", "parallel", "arbitrary")),
        name="sb_attn",
    )(q, k, v, bias)


def _sattn_kernel(pt_ref, qbd_ref, bias_ref, kn_ref, vn_ref, *refs, n_groups, group, n_heads, dh, n_new):
    del pt_ref
    k_refs = refs[:group]
    v_refs = refs[group:2 * group]
    o_ref, acc_ref, carry_ref = refs[2 * group:]
    g = pl.program_id(1)
    qbd = qbd_ref[...]
    bias = bias_ref[...]
    R, S = bias.shape
    su_ones, su_pair = _upper_and_ones(S), _strict_upper(2 * S)

    def blocks(kts, vts, valid):
        zs = [jnp.dot(qbd, kt, preferred_element_type=F32) + bias for kt in kts]
        a_list, carry = _sb_weights(zs, valid, carry_ref[...], su_ones, su_pair)
        carry_ref[...] = carry
        part = None
        for a, vt in zip(a_list, vts):
            av = lax.dot_general(a.astype(BF16), vt, (((1,), (1,)), ((), ())), preferred_element_type=F32)
            part = av if part is None else part + av
        acc_ref[...] += part

    @pl.when(g == 0)
    def _():
        acc_ref[...] = jnp.zeros_like(acc_ref)
        carry_ref[...] = jnp.zeros_like(carry_ref)
        t = lax.broadcasted_iota(jnp.int32, (R, S), 0) // n_heads
        s = lax.broadcasted_iota(jnp.int32, (R, S), 1)
        blocks([kn_ref[...]], [vn_ref[...]], [s < t])

    blocks([r[...].astype(BF16) for r in k_refs], [r[...].astype(BF16) for r in v_refs], None)

    @pl.when(g == n_groups - 1)
    def _():
        acc = acc_ref[...]
        row_h = lax.broadcasted_iota(jnp.int32, acc.shape, 0) % n_heads
        col_h = lax.broadcasted_iota(jnp.int32, acc.shape, 1) // dh
        own = jnp.where(row_h == col_h, acc, 0.0)
        o_ref[...] = jnp.sum(own.reshape(n_new, n_heads, acc.shape[1]), axis=1)


def _sattn(q, k_new, v_new, cache_k, cache_v, page_table, sb_bias, q_scale):
    B, T, D = q.shape
    H = sb_bias.shape[0]
    dh = D // H
    n_pool, S = cache_k.shape[0], cache_k.shape[1]
    n_pages = page_table.shape[1]
    group = max(g for g in (8, 4, 2, 1) if n_pages % g == 0)
    n_groups = n_pages // group
    R = T * H
    head_of_col = jnp.arange(D, dtype=jnp.int32) // dh
    own = head_of_col[None, None, :] == jnp.arange(H, dtype=jnp.int32)[None, :, None]
    qbd = jnp.where(own[:, None], (q * q_scale)[:, :, None, :], 0.0).reshape(B, R, D).astype(BF16)
    bias = jnp.broadcast_to(jnp.tile(-LOG2E * sb_bias.astype(F32), T)[:, None], (R, S))
    pad = ((0, 0), (0, S - T), (0, 0))
    kn = jnp.swapaxes(jnp.pad(k_new, pad), 1, 2).astype(BF16)
    vn = jnp.swapaxes(jnp.pad(v_new, pad), 1, 2).astype(BF16)
    ck = jnp.transpose(cache_k, (0, 2, 3, 1)).reshape(n_pool, D, S)
    cv = jnp.transpose(cache_v, (0, 2, 3, 1)).reshape(n_pool, D, S)

    def page_spec(idx):
        return pl.BlockSpec((None, D, S),
                            lambda b, g, pt: (pt[b, n_pages - 1 - (g * group + idx)], 0, 0))

    grid_spec = pltpu.PrefetchScalarGridSpec(
        num_scalar_prefetch=1,
        grid=(B, n_groups),
        in_specs=[pl.BlockSpec((None, R, D), lambda b, g, pt: (b, 0, 0)),
                  pl.BlockSpec((R, S), lambda b, g, pt: (0, 0)),
                  pl.BlockSpec((None, D, S), lambda b, g, pt: (b, 0, 0)),
                  pl.BlockSpec((None, D, S), lambda b, g, pt: (b, 0, 0))]
        + [page_spec(idx) for idx in range(group)] * 2,
        out_specs=pl.BlockSpec((None, T, D), lambda b, g, pt: (b, 0, 0)),
        scratch_shapes=[pltpu.VMEM((R, D), F32), pltpu.VMEM((R, S), F32)],
    )
    return pl.pallas_call(
        functools.partial(_sattn_kernel, n_groups=n_groups, group=group, n_heads=H, dh=dh, n_new=T),
        grid_spec=grid_spec,
        out_shape=jax.ShapeDtypeStruct((B, T, D), F32),
        compiler_params=_cparams(("parallel", "arbitrary")),
        name="sb_attn_paged",
    )(page_table, qbd, bias, kn, vn, *([ck] * group), *([cv] * group))


def _mlstm_layer(x, n_seq, rows_out, n_pad, chunk, conv_hist, state, wts, ln, alpha):
    w_up, conv_w, conv_b, w_qkv, wg, bg, gn_g, skip, w_down, n_heads = wts
    G, R, _ = x.shape
    I = conv_w.shape[1]
    L = G * R // n_seq
    up = _proj(x, w_up, BF16)
    up_seq = up.reshape(n_seq, L, 2 * I)
    if conv_hist is None:
        conv_src, zero_rows = up_seq, (L - SEQ_BLOCK, L - SEQ_BLOCK + n_pad)
    else:
        nh = conv_hist.shape[1]
        conv_src = jnp.concatenate([up_seq[:, :n_pad - nh, :I], conv_hist.astype(BF16),
                                    up_seq[:, n_pad:, :I]], axis=1)
        zero_rows = None
    xc = _conv_silu(conv_src.reshape(G, R, conv_src.shape[2]), conv_w, conv_b, zero_rows)
    qkv = _qkv_m(xc, up, w_qkv)
    gates = _gates(qkv, wg, bg, n_heads)
    hn, c_new, n_new, m_new = _mlstm(qkv.reshape(n_seq, L, 3 * I), gates.reshape(n_seq, L, GATE_LANES),
                                     n_heads, n_pad, chunk, state)
    x1 = _post_ln((hn.reshape(G, R, I), xc, up, gn_g, skip), x, w_down, ln[0], ln[1], alpha, rows_out,
                  gated=True)
    return x1, up_seq, (c_new, n_new, m_new)


def kernel(x_prompt, x_sample, state_C, state_n, state_m, state_conv, cache_k, cache_v, page_table,
           meta_tokens, w_up, conv_w, conv_b, w_q, w_k, w_v, w_gate, b_gate, gn_g, skip, w_down,
           w_qkv, w_o, sb_bias, mlp_w1, mlp_w2, ln_g, ln_b):
    B, SEQ, D = x_prompt.shape
    BD, T, _ = x_sample.shape
    depth = mlp_w1.shape[0]
    n_meta = meta_tokens.shape[0]
    I = conv_w.shape[2]
    width = conv_w.shape[1]
    HM = b_gate.shape[1] // 2
    HA = sb_bias.shape[1]
    alpha = (2.0 * depth) ** 0.25
    q_scale = -LOG2E * float(D // HA) ** -0.5
    S = SEQ_BLOCK
    assert SEQ % S == 0 and n_meta <= S and T + width - 1 <= SAMPLE_ROWS
    qb = ATTN_Q_BLOCKS
    while (SEQ // S) % qb:
        qb //= 2

    n_pad_p = S - n_meta
    LP = SEQ + S
    xp = jnp.concatenate([x_prompt, jnp.zeros((B, n_pad_p, D), F32),
                          jnp.broadcast_to(meta_tokens.astype(F32)[None], (B, n_meta, D))], axis=1)
    n_pad_s = SAMPLE_ROWS - T
    RS = BD * SAMPLE_ROWS
    xs = jnp.pad(x_sample, ((0, 0), (n_pad_s, 0), (0, 0))).reshape(1, RS, D)

    outs = {name: [] for name in ("p_C", "p_n", "p_m", "p_conv", "p_k", "p_v",
                                  "s_C", "s_n", "s_m", "s_conv", "s_k", "s_v")}
    for i in range(depth):
        j = i // 2
        last = i == depth - 1
        rows_p = SEQ if last else LP
        lng, lnb = ln_g[i], ln_b[i]
        w1, w2 = mlp_w1[i].astype(BF16), mlp_w2[i].astype(BF16)
        if i % 2 == 0:
            wg = jnp.pad(w_gate[j], ((0, 0), (0, GATE_LANES - 2 * HM))).astype(BF16)
            bg = jnp.pad(b_gate[j], (0, GATE_LANES - 2 * HM)).reshape(1, GATE_LANES).astype(F32)
            wts = (w_up[j].astype(BF16), conv_w[j], conv_b[j],
                   jnp.concatenate([w_q[j], w_k[j], w_v[j]], axis=1).astype(BF16), wg, bg,
                   gn_g[j], skip[j], w_down[j].astype(BF16), HM)
            xp1, up_p, (Cp, np_, mp) = _mlstm_layer(xp, B, rows_p, n_pad_p, S, None, None, wts,
                                                    (lng[0], lnb[0]), alpha)
            xs1, up_s, (Cs, ns, ms) = _mlstm_layer(xs, BD, RS, n_pad_s, SAMPLE_ROWS, state_conv[j],
                                                   (state_C[j], state_n[j], state_m[j]), wts,
                                                   (lng[0], lnb[0]), alpha)
            outs["p_C"].append(Cp); outs["p_n"].append(np_); outs["p_m"].append(mp)
            outs["p_conv"].append(up_p[:, SEQ - (width - 1):SEQ, :I].astype(F32))
            outs["s_C"].append(Cs); outs["s_n"].append(ns); outs["s_m"].append(ms)
            hist = jnp.concatenate([state_conv[j], up_s[:, n_pad_s:, :I].astype(F32)], axis=1)
            outs["s_conv"].append(hist[:, T:])
        else:
            wqkv = w_qkv[j].astype(BF16)
            wo = w_o[j].astype(BF16)
            q, k, v, pk, pv = _qkv_a(xp, wqkv, SEQ, n_meta, q_scale)
            o = _attn(q, k, v, sb_bias[j], n_pad_p, qb, 1, SEQ // (qb * S))
            if not last:
                o_meta = _attn(q, k, v, sb_bias[j], n_pad_p, 1, 0, 1)
                o = jnp.concatenate([o, o_meta], axis=1)
            xp1 = _post_ln(o, xp, wo, lng[0], lnb[0], alpha, rows_p, gated=False)
            outs["p_k"].append(pk.reshape(B, n_meta + SEQ, HA, D // HA))
            outs["p_v"].append(pv.reshape(B, n_meta + SEQ, HA, D // HA))
            qkv_s = _proj(xs, wqkv, F32).reshape(BD, SAMPLE_ROWS, 3 * D)[:, n_pad_s:]
            qs, ks, vs = qkv_s[..., :D], qkv_s[..., D:2 * D], qkv_s[..., 2 * D:]
            o_s = _sattn(qs, ks, vs, cache_k[j], cache_v[j], page_table, sb_bias[j], q_scale)
            o_s = jnp.pad(o_s, ((0, 0), (n_pad_s, 0), (0, 0))).astype(BF16).reshape(1, RS, D)
            xs1 = _post_ln(o_s, xs, wo, lng[0], lnb[0], alpha, RS, gated=False)
            outs["s_k"].append(ks.reshape(BD, T, HA, D // HA))
            outs["s_v"].append(vs.reshape(BD, T, HA, D // HA))
        xp = _mlp_ln(xp1, w1, w2, lng[1], lnb[1], alpha, rows_p)
        xs = _mlp_ln(xs1, w1, w2, lng[1], lnb[1], alpha, RS)
    y_prompt = xp[:, :SEQ]
    y_sample = xs.reshape(BD, SAMPLE_ROWS, D)[:, n_pad_s:]
    st = lambda name: jnp.stack(outs[name])
    return (y_prompt, y_sample, st("p_C"), st("p_n"), st("p_m"), st("p_conv"), st("p_k"), st("p_v"),
            st("s_C"), st("s_n"), st("s_m"), st("s_conv"), st("s_k"), st("s_v"))
```

```python
import functools

import jax
import jax.numpy as jnp
from jax import lax
from jax.experimental import pallas as pl
from jax.experimental.pallas import tpu as pltpu

F32 = jnp.float32
BF16 = jnp.bfloat16

LN_EPS = 1e-5
GN_EPS = 1e-6
NEG = -1e30
LOG2E = 1.4426950408889634
SEQ_BLOCK = 128
SAMPLE_ROWS = 16
GATE_LANES = 128
ATTN_Q_BLOCKS = 16
MLSTM_HEAD_GROUP = 4
VMEM_LIMIT = 56 * 1024 * 1024


def _cparams(sem):
    return pltpu.CompilerParams(dimension_semantics=sem, vmem_limit_bytes=VMEM_LIMIT)


def _row_tile(n, cap):
    best = None
    for t in range(16, min(n, cap) + 1, 16):
        if n % t == 0:
            best = t
    assert best is not None, (n, cap)
    return best


def _col_tile(n, cap):
    best = None
    for t in range(128, min(n, cap) + 1, 128):
        if n % t == 0:
            best = t
    assert best is not None, (n, cap)
    return best


def _sigmoid(x):
    return 1.0 / (1.0 + jnp.exp(-x))


def _softplus(x):
    return jnp.maximum(x, 0.0) + jnp.log1p(jnp.exp(-jnp.abs(x)))


def _layer_norm(y, g, b):
    mu = jnp.mean(y, axis=-1, keepdims=True)
    yc = y - mu
    var = jnp.mean(yc * yc, axis=-1, keepdims=True)
    return yc * lax.rsqrt(var + LN_EPS) * g + b


def _proj_kernel(x_ref, w_ref, o_ref):
    o_ref[...] = jnp.dot(x_ref[...].astype(BF16), w_ref[...],
                         preferred_element_type=F32).astype(o_ref.dtype)


def _proj(x, w, out_dtype, tm_cap=1088, tn_cap=1024):
    B, L, K = x.shape
    N = w.shape[1]
    tm, tn = _row_tile(L, tm_cap), _col_tile(N, tn_cap)
    return pl.pallas_call(
        _proj_kernel,
        grid=(B, L // tm, N // tn),
        in_specs=[pl.BlockSpec((None, tm, K), lambda b, i, j: (b, i, 0)),
                  pl.BlockSpec((K, tn), lambda b, i, j: (0, j))],
        out_specs=pl.BlockSpec((None, tm, tn), lambda b, i, j: (b, i, j)),
        out_shape=jax.ShapeDtypeStruct((B, L, N), out_dtype),
        compiler_params=_cparams(("parallel", "parallel", "arbitrary")),
        name="proj",
    )(x, w)


def _qkv_m_kernel(xc_ref, xm_ref, w_ref, o_ref, *, n_qk):
    j = pl.program_id(2)

    @pl.when(j < n_qk)
    def _():
        o_ref[...] = jnp.dot(xc_ref[...], w_ref[...], preferred_element_type=F32).astype(o_ref.dtype)

    @pl.when(j >= n_qk)
    def _():
        o_ref[...] = jnp.dot(xm_ref[...], w_ref[...], preferred_element_type=F32).astype(o_ref.dtype)


def _qkv_m(xc, xm_src, w_qkv):
    B, L, I = xc.shape
    tm, tn = _row_tile(L, 1088), _col_tile(I, 1024)
    return pl.pallas_call(
        functools.partial(_qkv_m_kernel, n_qk=2 * I // tn),
        grid=(B, L // tm, 3 * I // tn),
        in_specs=[pl.BlockSpec((None, tm, I), lambda b, i, j: (b, i, 0)),
                  pl.BlockSpec((None, tm, I), lambda b, i, j: (b, i, 0)),
                  pl.BlockSpec((I, tn), lambda b, i, j: (0, j))],
        out_specs=pl.BlockSpec((None, tm, tn), lambda b, i, j: (b, i, j)),
        out_shape=jax.ShapeDtypeStruct((B, L, 3 * I), BF16),
        compiler_params=_cparams(("parallel", "parallel", "arbitrary")),
        name="qkv_m",
    )(xc, xm_src, w_qkv)


def _gates_kernel(x_ref, w_ref, b_ref, o_ref, *, n_heads):
    g = jnp.dot(x_ref[...], w_ref[...], preferred_element_type=F32) + b_ref[...]
    lane = lax.broadcasted_iota(jnp.int32, g.shape, 1)
    o_ref[...] = jnp.where(lane < n_heads, g, -_softplus(-g))


def _gates(qkv, wg, bg, n_heads):
    B, L, K = qkv.shape
    tm = _row_tile(L, 544)
    return pl.pallas_call(
        functools.partial(_gates_kernel, n_heads=n_heads),
        grid=(B, L // tm),
        in_specs=[pl.BlockSpec((None, tm, K), lambda b, i: (b, i, 0)),
                  pl.BlockSpec((K, GATE_LANES), lambda b, i: (0, 0)),
                  pl.BlockSpec((1, GATE_LANES), lambda b, i: (0, 0))],
        out_specs=pl.BlockSpec((None, tm, GATE_LANES), lambda b, i: (b, i, 0)),
        out_shape=jax.ShapeDtypeStruct((B, L, GATE_LANES), F32),
        compiler_params=_cparams(("parallel", "parallel")),
        name="gates",
    )(qkv, wg, bg)


def _conv_kernel(x_ref, w_ref, b_ref, o_ref, *, width, zero_rows):
    x = x_ref[...].astype(F32)
    if zero_rows is not None:
        row = lax.broadcasted_iota(jnp.int32, x.shape, 0)
        x = jnp.where((row >= zero_rows[0]) & (row < zero_rows[1]), 0.0, x)
    acc = b_ref[...] + w_ref[width - 1:width, :] * x
    for j in range(width - 1):
        acc = acc + w_ref[j:j + 1, :] * pltpu.roll(x, width - 1 - j, axis=0)
    o_ref[...] = (acc * _sigmoid(acc)).astype(o_ref.dtype)


def _conv_silu(x_src, conv_w, conv_b, zero_rows):
    B, L, _ = x_src.shape
    width, I = conv_w.shape
    tc = _col_tile(I, 256)
    return pl.pallas_call(
        functools.partial(_conv_kernel, width=width, zero_rows=zero_rows),
        grid=(B, I // tc),
        in_specs=[pl.BlockSpec((None, L, tc), lambda b, j: (b, 0, j)),
                  pl.BlockSpec((width, tc), lambda b, j: (0, j)),
                  pl.BlockSpec((1, tc), lambda b, j: (0, j))],
        out_specs=pl.BlockSpec((None, L, tc), lambda b, j: (b, 0, j)),
        out_shape=jax.ShapeDtypeStruct((B, L, I), BF16),
        compiler_params=_cparams(("parallel", "parallel")),
        name="conv_silu",
    )(x_src, conv_w, conv_b.reshape(1, I))


def _mlstm_kernel(*refs, n_heads, n_chunks, n_pad, scale, has_state):
    if has_state:
        (q_ref, k_ref, v_ref, g_ref, gt_ref, c0_ref, n0_ref, m0_ref,
         h_ref, c_out, n_out, m_out, c_sc, n_sc, m_sc) = refs
    else:
        (q_ref, k_ref, v_ref, g_ref, gt_ref,
         h_ref, c_out, n_out, m_out, c_sc, n_sc, m_sc) = refs
    c = pl.program_id(1)
    L = q_ref.shape[0]
    dh = q_ref.shape[1] // n_heads

    @pl.when(c == 0)
    def _():
        if has_state:
            c_sc[...] = c0_ref[...]
            n_sc[...] = n0_ref[...]
            m_sc[...] = m0_ref[...]
        else:
            c_sc[...] = jnp.zeros_like(c_sc)
            n_sc[...] = jnp.zeros_like(n_sc)
            m_sc[...] = jnp.zeros_like(m_sc)

    n_pad_c = jnp.where(c == 0, n_pad, 0)
    g = g_ref[...]
    row_g = lax.broadcasted_iota(jnp.int32, g.shape, 0)
    lane_g = lax.broadcasted_iota(jnp.int32, g.shape, 1)
    g = jnp.where(row_g < n_pad_c, jnp.where(lane_g < n_heads, NEG, 0.0), g)
    gt = gt_ref[...]
    sub_t = lax.broadcasted_iota(jnp.int32, gt.shape, 0)
    lane_t = lax.broadcasted_iota(jnp.int32, gt.shape, 1)
    gt = jnp.where(lane_t < n_pad_c, jnp.where(sub_t < n_heads, NEG, 0.0), gt)

    r = lax.broadcasted_iota(jnp.int32, (L, L), 0)
    s = lax.broadcasted_iota(jnp.int32, (L, L), 1)
    causal = s <= r
    tri = causal.astype(F32)
    cum_col = jnp.dot(tri, jnp.where(lane_g >= n_heads, g, 0.0),
                      preferred_element_type=F32, precision=lax.Precision.HIGHEST)
    cum_row = lax.dot_general(jnp.where(sub_t >= n_heads, gt, 0.0), tri, (((1,), (1,)), ((), ())),
                              preferred_element_type=F32, precision=lax.Precision.HIGHEST)

    def head_group(heads):
        cols = {hd: slice(hd * dh, (hd + 1) * dh) for hd in heads}
        b_col = {hd: cum_col[:, n_heads + hd:n_heads + hd + 1] for hd in heads}
        ig_col = {hd: g[:, hd:hd + 1] for hd in heads}
        b_row = {hd: cum_row[n_heads + hd:n_heads + hd + 1, :] for hd in heads}
        ig_row = {hd: gt[hd:hd + 1, :] for hd in heads}
        b_end = {hd: jnp.sum(gt[n_heads + hd:n_heads + hd + 1, :], axis=1, keepdims=True) for hd in heads}
        m = {hd: m_sc[hd][:, 0:1] for hd in heads}
        q = {h: q_ref[:, cols[h]] for h in heads}
        k = {h: k_ref[:, cols[h]] for h in heads}
        v = {h: v_ref[:, cols[h]] for h in heads}

        nt = (((1,), (1,)), ((), ()))
        qk = {h: lax.dot_general(q[h], k[h], nt, preferred_element_type=F32) for h in heads}
        qc = {h: jnp.dot(q[h], c_sc[h].astype(BF16), preferred_element_type=F32) for h in heads}
        d = {h: jnp.where(causal, b_col[h] - b_row[h] + ig_row[h], NEG) for h in heads}
        m_inter = {h: b_col[h] + m[h] for h in heads}
        m_t = {h: jnp.maximum(m_inter[h], jnp.max(d[h], axis=1, keepdims=True)) for h in heads}
        sw = {h: qk[h] * (scale * jnp.exp(d[h] - m_t[h])) for h in heads}
        sc = {h: jnp.exp(m_inter[h] - m_t[h]) for h in heads}
        num = {h: jnp.dot(sw[h].astype(BF16), v[h], preferred_element_type=F32) + sc[h] * qc[h] for h in heads}
        den = {h: jnp.sum(sw[h], axis=1, keepdims=True)
               + sc[h] * jnp.sum(q[h].astype(F32) * n_sc[h], axis=1, keepdims=True) for h in heads}
        hh = {h: num[h] * (1.0 / jnp.maximum(jnp.abs(den[h]), jnp.exp(-m_t[h]))) for h in heads}
        mu = {h: jnp.mean(hh[h], axis=1, keepdims=True) for h in heads}
        hc = {h: hh[h] - mu[h] for h in heads}
        var = {h: jnp.mean(hc[h] * hc[h], axis=1, keepdims=True) for h in heads}
        for h in heads:
            h_ref[:, cols[h]] = (hc[h] * lax.rsqrt(var[h] + GN_EPS)).astype(h_ref.dtype)

        g_col = {h: b_end[h] - b_col[h] + ig_col[h] for h in heads}
        g_row = {h: b_end[h] - b_row[h] + ig_row[h] for h in heads}
        m_new = {h: jnp.maximum(b_end[h] + m[h], jnp.max(g_row[h], axis=1, keepdims=True)) for h in heads}
        w_col = {h: scale * jnp.exp(g_col[h] - m_new[h]) for h in heads}
        decay = {h: jnp.exp(b_end[h] + m[h] - m_new[h]) for h in heads}
        vw = {h: (v[h].astype(F32) * w_col[h]).astype(BF16) for h in heads}
        tn = (((0,), (0,)), ((), ()))
        kv = {h: lax.dot_general(k[h], vw[h], tn, preferred_element_type=F32) for h in heads}
        for h in heads:
            c_sc[h] = decay[h] * c_sc[h] + kv[h]
            n_sc[h] = decay[h] * n_sc[h] + jnp.sum(k[h].astype(F32) * w_col[h], axis=0, keepdims=True)
            m_sc[h] = jnp.broadcast_to(m_new[h], m_sc.shape[1:])

    for h0 in range(0, n_heads, MLSTM_HEAD_GROUP):
        head_group(range(h0, min(h0 + MLSTM_HEAD_GROUP, n_heads)))

    @pl.when(c == n_chunks - 1)
    def _():
        c_out[...] = c_sc[...]
        n_out[...] = n_sc[...]
        m_out[...] = m_sc[...]


def _mlstm(qkv, gates, n_heads, n_pad, chunk, state=None):
    B, L, I3 = qkv.shape
    I = I3 // 3
    dh = I // n_heads
    nc = L // chunk
    assert 2 * n_heads <= 8
    gates_t = jnp.swapaxes(gates[:, :, :8], 1, 2)
    has_state = state is not None

    def blk(c):
        return (c + nc - 1) % nc

    state_specs = [pl.BlockSpec((None, n_heads, dh, dh), lambda b, c: (b, 0, 0, 0)),
                   pl.BlockSpec((None, n_heads, 1, dh), lambda b, c: (b, 0, 0, 0)),
                   pl.BlockSpec((None, n_heads, 1, GATE_LANES), lambda b, c: (b, 0, 0, 0))]
    in_specs = [pl.BlockSpec((None, chunk, I), lambda b, c: (b, blk(c), 0)),
                pl.BlockSpec((None, chunk, I), lambda b, c: (b, blk(c), 1)),
                pl.BlockSpec((None, chunk, I), lambda b, c: (b, blk(c), 2)),
                pl.BlockSpec((None, chunk, GATE_LANES), lambda b, c: (b, blk(c), 0)),
                pl.BlockSpec((None, 8, chunk), lambda b, c: (b, 0, blk(c)))]
    args = [qkv, qkv, qkv, gates, gates_t]
    if has_state:
        c0, n0, m0 = state
        in_specs += state_specs
        args += [c0, n0.reshape(B, n_heads, 1, dh),
                 jnp.broadcast_to(m0.reshape(B, n_heads, 1, 1), (B, n_heads, 1, GATE_LANES))]
    out_shape = (jax.ShapeDtypeStruct((B, L, I), BF16),
                 jax.ShapeDtypeStruct((B, n_heads, dh, dh), F32),
                 jax.ShapeDtypeStruct((B, n_heads, 1, dh), F32),
                 jax.ShapeDtypeStruct((B, n_heads, 1, GATE_LANES), F32))
    out_specs = [pl.BlockSpec((None, chunk, I), lambda b, c: (b, blk(c), 0))] + state_specs
    hn, c_new, n_new, m_new = pl.pallas_call(
        functools.partial(_mlstm_kernel, n_heads=n_heads, n_chunks=nc, n_pad=n_pad,
                          scale=float(dh) ** -0.5, has_state=has_state),
        grid=(B, nc),
        in_specs=in_specs,
        out_specs=out_specs,
        out_shape=out_shape,
        scratch_shapes=[pltpu.VMEM((n_heads, dh, dh), F32), pltpu.VMEM((n_heads, 1, dh), F32),
                        pltpu.VMEM((n_heads, 1, GATE_LANES), F32)],
        compiler_params=_cparams(("parallel", "arbitrary")),
        name="mlstm",
    )(*args)
    return hn, c_new, n_new[:, :, 0, :], m_new[:, :, 0, 0]


def _post_ln_kernel(*refs, gated, alpha):
    if gated:
        hn_ref, xc_ref, z_ref, gn_ref, sk_ref, res_ref, w_ref, g_ref, b_ref, o_ref = refs
        z = z_ref[...].astype(F32)
        hcomb = hn_ref[...].astype(F32) * gn_ref[...] + sk_ref[...] * xc_ref[...].astype(F32)
        lhs = (hcomb * (z * _sigmoid(z))).astype(BF16)
    else:
        lhs_ref, res_ref, w_ref, g_ref, b_ref, o_ref = refs
        lhs = lhs_ref[...]
    mix = jnp.dot(lhs, w_ref[...], preferred_element_type=F32)
    y = alpha * res_ref[...] + mix
    o_ref[...] = _layer_norm(y, g_ref[...], b_ref[...])


def _post_ln(lhs_args, res, w, ln_g, ln_b, alpha, rows, gated):
    B = res.shape[0]
    K, D = w.shape
    tm = _row_tile(rows, 544)
    row = lambda b, i: (b, i, 0)
    const = lambda b, i: (0, 0)
    if gated:
        hn, xc, z_src, gn_g, skip = lhs_args
        nz = z_src.shape[2] // K - 1
        in_specs = [pl.BlockSpec((None, tm, K), row),
                    pl.BlockSpec((None, tm, K), row),
                    pl.BlockSpec((None, tm, K), lambda b, i: (b, i, nz)),
                    pl.BlockSpec((1, K), const),
                    pl.BlockSpec((1, K), const)]
        args = [hn, xc, z_src, gn_g.reshape(1, K), skip.reshape(1, K)]
    else:
        in_specs = [pl.BlockSpec((None, tm, K), row)]
        args = [lhs_args]
    in_specs += [pl.BlockSpec((None, tm, D), row),
                 pl.BlockSpec((K, D), const),
                 pl.BlockSpec((1, D), const),
                 pl.BlockSpec((1, D), const)]
    args += [res, w, ln_g.reshape(1, D), ln_b.reshape(1, D)]
    return pl.pallas_call(
        functools.partial(_post_ln_kernel, gated=gated, alpha=alpha),
        grid=(B, rows // tm),
        in_specs=in_specs,
        out_specs=pl.BlockSpec((None, tm, D), row),
        out_shape=jax.ShapeDtypeStruct((B, rows, D), F32),
        compiler_params=_cparams(("parallel", "parallel")),
        name="post_ln_gated" if gated else "post_ln",
    )(*args)


def _mlp_kernel(x_ref, w1_ref, w2_ref, g_ref, b_ref, o_ref, *, alpha, n_chunks):
    x = x_ref[...]
    xb = x.astype(BF16)
    tf = w1_ref.shape[1] // n_chunks
    mix = None
    for c in range(n_chunks):
        hid = jnp.maximum(jnp.dot(xb, w1_ref[:, c * tf:(c + 1) * tf], preferred_element_type=F32), 0.0)
        part = jnp.dot((hid * hid).astype(BF16), w2_ref[c * tf:(c + 1) * tf, :], preferred_element_type=F32)
        mix = part if mix is None else mix + part
    o_ref[...] = _layer_norm(alpha * x + mix, g_ref[...], b_ref[...])


def _mlp_ln(x, w1, w2, ln_g, ln_b, alpha, rows):
    B, _, D = x.shape
    F = w1.shape[1]
    tm = _row_tile(rows, 544)
    once = pl.Buffered(1)
    return pl.pallas_call(
        functools.partial(_mlp_kernel, alpha=alpha, n_chunks=F // _col_tile(F, 2048)),
        grid=(B, rows // tm),
        in_specs=[pl.BlockSpec((None, tm, D), lambda b, i: (b, i, 0)),
                  pl.BlockSpec((D, F), lambda b, i: (0, 0), pipeline_mode=once),
                  pl.BlockSpec((F, D), lambda b, i: (0, 0), pipeline_mode=once),
                  pl.BlockSpec((1, D), lambda b, i: (0, 0)),
                  pl.BlockSpec((1, D), lambda b, i: (0, 0))],
        out_specs=pl.BlockSpec((None, tm, D), lambda b, i: (b, i, 0)),
        out_shape=jax.ShapeDtypeStruct((B, rows, D), F32),
        compiler_params=_cparams(("parallel", "parallel")),
        name="mlp_ln",
    )(x, w1, w2, ln_g.reshape(1, D), ln_b.reshape(1, D))


def _qkv_a_kernel(x_ref, wq_ref, wk_ref, wv_ref, q_ref, k_ref, v_ref, pk_ref, pv_ref, *,
                  seq, n_meta, q_scale):
    x = x_ref[...].astype(BF16)
    L = x.shape[0]
    q = jnp.dot(x, wq_ref[...], preferred_element_type=F32)
    q_ref[...] = (q * q_scale).astype(q_ref.dtype)
    for w_ref, o_ref, p_ref in ((wk_ref, k_ref, pk_ref), (wv_ref, v_ref, pv_ref)):
        y = jnp.dot(x, w_ref[...], preferred_element_type=F32)
        o_ref[...] = y.astype(o_ref.dtype)
        p_ref[0:n_meta, :] = y[L - n_meta:L, :]
        p_ref[n_meta:n_meta + seq, :] = y[0:seq, :]


def _qkv_a(x, w_qkv, seq, n_meta, q_scale):
    B, L, D = x.shape
    tn = _col_tile(D, 256)
    nj = D // tn
    xs = pl.BlockSpec((None, L, D), lambda b, j: (b, 0, 0))
    os = pl.BlockSpec((None, L, tn), lambda b, j: (b, 0, j))
    ps = pl.BlockSpec((None, n_meta + seq, tn), lambda b, j: (b, 0, j))
    return pl.pallas_call(
        functools.partial(_qkv_a_kernel, seq=seq, n_meta=n_meta, q_scale=q_scale),
        grid=(B, nj),
        in_specs=[xs,
                  pl.BlockSpec((D, tn), lambda b, j: (0, j)),
                  pl.BlockSpec((D, tn), lambda b, j: (0, nj + j)),
                  pl.BlockSpec((D, tn), lambda b, j: (0, 2 * nj + j))],
        out_specs=(os, os, os, ps, ps),
        out_shape=(jax.ShapeDtypeStruct((B, L, D), BF16),) * 3
        + (jax.ShapeDtypeStruct((B, n_meta + seq, D), F32),) * 2,
        compiler_params=_cparams(("parallel", "arbitrary")),
        name="qkv_a",
    )(x, w_qkv, w_qkv, w_qkv)


def _upper_and_ones(n):
    j = lax.broadcasted_iota(jnp.int32, (n, 2 * n), 0)
    s = lax.broadcasted_iota(jnp.int32, (n, 2 * n), 1)
    return ((j > s) | (s >= n)).astype(BF16)


def _strict_upper(n):
    j = lax.broadcasted_iota(jnp.int32, (n, n), 0)
    s = lax.broadcasted_iota(jnp.int32, (n, n), 1)
    return (j > s).astype(BF16)


def _sb_weights(zns, valid, carry, su_ones, su_pair):
    S = zns[0].shape[1]
    valid = [None] * len(zns) if valid is None else valid
    lks = []
    for zn, ok in zip(zns, valid):
        lk = jnp.minimum(zn, 0.0) - LOG2E * jnp.log(1.0 + jnp.exp2(-jnp.abs(zn)))
        lks.append(lk if ok is None else jnp.where(ok, lk, 0.0))
    n_pairs = len(zns) // 2
    afters, totals = [], []
    for p in range(n_pairs):
        late, early = lks[2 * p], lks[2 * p + 1]
        res = jnp.dot(jnp.concatenate([early, late], axis=1).astype(BF16), su_pair, preferred_element_type=F32)
        afters += [res[:, S:], res[:, :S]]
        totals += [None, res[:, 0:1] + early[:, 0:1]]
    for lk in lks[2 * n_pairs:]:
        res = jnp.dot(lk.astype(BF16), su_ones, preferred_element_type=F32)
        afters.append(res[:, :S])
        totals.append(res[:, S:])
    out = []
    for zn, lk, after, total, ok in zip(zns, lks, afters, totals, valid):
        a = jnp.exp2((lk - zn) + (carry + after))
        out.append(a if ok is None else jnp.where(ok, a, 0.0))
        if total is not None:
            carry = carry + total
    return out, carry


def _attn_kernel(q_ref, k_ref, v_ref, bias_ref, o_ref, carry_sc, acc_sc, *,
                 qb, first_block, n_blocks, n_pad, dh):
    S = SEQ_BLOCK
    lb0 = first_block + pl.program_id(2) * qb
    first = lax.broadcasted_iota(jnp.int32, (S, 2 * dh), 1) < dh
    zero = jnp.zeros((S, 2 * dh), BF16)

    def tall(x):
        return jnp.concatenate([jnp.where(first, x, zero), jnp.where(first, zero, x)], axis=0)

    q_tall = jnp.concatenate([tall(q_ref[r * S:(r + 1) * S, :]) for r in range(qb)], axis=0)
    su_ones, su_pair = _upper_and_ones(S), _strict_upper(2 * S)
    carry_sc[...] = jnp.zeros_like(carry_sc)
    acc_sc[...] = jnp.zeros_like(acc_sc)

    def key_tile(j, n, r0, diag, pad):
        nr = qb - r0
        rows = slice(r0 * 2 * S, qb * 2 * S)
        start = ((j - n + n_blocks) % n_blocks) * S
        if not isinstance(j, int):
            start = pl.multiple_of(start, S)
        z_all = lax.dot_general(q_tall[rows], k_ref[pl.ds(start, n * S), :], (((1,), (1,)), ((), ())),
                                preferred_element_type=F32)
        bias = bias_ref[rows, :]
        v_all = v_ref[pl.ds(start, n * S), :]
        zs = [z_all[:, c * S:(c + 1) * S] + bias for c in reversed(range(n))]
        vts = [tall(v_all[c * S:(c + 1) * S, :]) for c in reversed(range(n))]
        valid = None
        if diag or pad:
            riota = lax.broadcasted_iota(jnp.int32, (nr * 2 * S, S), 0)
            col = lax.broadcasted_iota(jnp.int32, (nr * 2 * S, S), 1)
            valid = [None] * n
            if diag:
                inside = col < (riota & (S - 1))
                valid = [(riota >= (c + 1) * 2 * S) | ((riota >= c * 2 * S) & inside) for c in reversed(range(n))]
            if pad:
                valid = [(col >= n_pad) if v is None else v & (col >= n_pad) for v in valid]
        a_list, carry = _sb_weights(zs, valid, carry_sc[rows, :], su_ones, su_pair)
        carry_sc[rows, :] = carry

        def wide(a):
            a = a.astype(BF16)
            return jnp.concatenate(
                [jnp.concatenate([a[r * 2 * S:r * 2 * S + S], a[r * 2 * S + S:(r + 1) * 2 * S]], axis=1)
                 for r in range(nr)], axis=0)

        a_wide = jnp.concatenate([wide(a) for a in a_list], axis=1)
        acc_sc[r0 * S:qb * S, :] += jnp.dot(a_wide, jnp.concatenate(vts, axis=0), preferred_element_type=F32)

    nd = 2 if qb % 2 == 0 else 1
    for t in range(qb // nd):
        r0 = qb - nd * (t + 1)
        key_tile(lb0 + r0 + nd - 1, nd, r0, diag=True, pad=first_block == 0)
    if first_block > 0:
        n = 4 if qb % 4 == 0 else 2 if qb % 2 == 0 else 1

        def full(t, _):
            key_tile(lb0 - 1 - n * t, n, 0, diag=False, pad=False)
            return 0

        lax.fori_loop(0, (lb0 - 1) // n, full, 0)
        key_tile(0, 1, 0, diag=False, pad=True)
    o_ref[...] = acc_sc[...].astype(o_ref.dtype)


def _attn(q, k, v, sb_bias, n_pad, qb, first_block, n_tiles):
    B, L, D = q.shape
    H = sb_bias.shape[0]
    dh = D // H
    assert 2 * dh == SEQ_BLOCK
    S = SEQ_BLOCK
    R = qb * S
    nb = L // S
    bias = jnp.broadcast_to((-LOG2E * sb_bias.astype(F32)).reshape(H // 2, 1, 2, 1, 1), (H // 2, qb, 2, S, S))
    bias = bias.reshape(H // 2, 2 * R, S)
    q_index = lambda b, p, i: (b, (first_block + nb - 1) % nb // qb + i, p)
    return pl.pallas_call(
        functools.partial(_attn_kernel, qb=qb, first_block=first_block, n_blocks=nb, n_pad=n_pad, dh=dh),
        grid=(B, H // 2, n_tiles),
        in_specs=[pl.BlockSpec((None, R, 2 * dh), q_index),
                  pl.BlockSpec((None, L, 2 * dh), lambda b, p, i: (b, 0, p)),
                  pl.BlockSpec((None, L, 2 * dh), lambda b, p, i: (b, 0, p)),
                  pl.BlockSpec((None, 2 * R, S), lambda b, p, i: (p, 0, 0))],
        out_specs=pl.BlockSpec((None, R, 2 * dh), lambda b, p, i: (b, i, p)),
        out_shape=jax.ShapeDtypeStruct((B, n_tiles * R, D), BF16),
        scratch_shapes=[pltpu.VMEM((2 * R, S), F32), pltpu.VMEM((R, 2 * dh), F32)],
        compiler_params=_cparams(("parallel", "parallel", "arbitrary")),
        name="sb_attn",
    )(q, k, v, bias)


def _sattn_kernel(pt_ref, qbd_ref, bias_ref, kn_ref, vn_ref, *refs, n_groups, group, n_heads, dh, n_new):
    del pt_ref
    k_refs = refs[:group]
    v_refs = refs[group:2 * group]
    o_ref, acc_ref, carry_ref = refs[2 * group:]
    g = pl.program_id(1)
    qbd = qbd_ref[...]
    bias = bias_ref[...]
    R, S = bias.shape
    su_ones, su_pair = _upper_and_ones(S), _strict_upper(2 * S)

    def blocks(kts, vts, valid):
        zs = [jnp.dot(qbd, kt, preferred_element_type=F32) + bias for kt in kts]
        a_list, carry = _sb_weights(zs, valid, carry_ref[...], su_ones, su_pair)
        carry_ref[...] = carry
        part = None
        for a, vt in zip(a_list, vts):
            av = lax.dot_general(a.astype(BF16), vt, (((1,), (1,)), ((), ())), preferred_element_type=F32)
            part = av if part is None else part + av
        acc_ref[...] += part

    @pl.when(g == 0)
    def _():
        acc_ref[...] = jnp.zeros_like(acc_ref)
        carry_ref[...] = jnp.zeros_like(carry_ref)
        t = lax.broadcasted_iota(jnp.int32, (R, S), 0) // n_heads
        s = lax.broadcasted_iota(jnp.int32, (R, S), 1)
        blocks([kn_ref[...]], [vn_ref[...]], [s < t])

    blocks([r[...].astype(BF16) for r in k_refs], [r[...].astype(BF16) for r in v_refs], None)

    @pl.when(g == n_groups - 1)
    def _():
        acc = acc_ref[...]
        row_h = lax.broadcasted_iota(jnp.int32, acc.shape, 0) % n_heads
        col_h = lax.broadcasted_iota(jnp.int32, acc.shape, 1) // dh
        own = jnp.where(row_h == col_h, acc, 0.0)
        o_ref[...] = jnp.sum(own.reshape(n_new, n_heads, acc.shape[1]), axis=1)


def _sattn(q, k_new, v_new, cache_k, cache_v, page_table, sb_bias, q_scale):
    B, T, D = q.shape
    H = sb_bias.shape[0]
    dh = D // H
    n_pool, S = cache_k.shape[0], cache_k.shape[1]
    n_pages = page_table.shape[1]
    group = max(g for g in (8, 4, 2, 1) if n_pages % g == 0)
    n_groups = n_pages // group
    R = T * H
    head_of_col = jnp.arange(D, dtype=jnp.int32) // dh
    own = head_of_col[None, None, :] == jnp.arange(H, dtype=jnp.int32)[None, :, None]
    qbd = jnp.where(own[:, None], (q * q_scale)[:, :, None, :], 0.0).reshape(B, R, D).astype(BF16)
    bias = jnp.broadcast_to(jnp.tile(-LOG2E * sb_bias.astype(F32), T)[:, None], (R, S))
    pad = ((0, 0), (0, S - T), (0, 0))
    kn = jnp.swapaxes(jnp.pad(k_new, pad), 1, 2).astype(BF16)
    vn = jnp.swapaxes(jnp.pad(v_new, pad), 1, 2).astype(BF16)
    ck = jnp.transpose(cache_k, (0, 2, 3, 1)).reshape(n_pool, D, S)
    cv = jnp.transpose(cache_v, (0, 2, 3, 1)).reshape(n_pool, D, S)

    def page_spec(idx):
        return pl.BlockSpec((None, D, S),
                            lambda b, g, pt: (pt[b, n_pages - 1 - (g * group + idx)], 0, 0))

    grid_spec = pltpu.PrefetchScalarGridSpec(
        num_scalar_prefetch=1,
        grid=(B, n_groups),
        in_specs=[pl.BlockSpec((None, R, D), lambda b, g, pt: (b, 0, 0)),
                  pl.BlockSpec((R, S), lambda b, g, pt: (0, 0)),
                  pl.BlockSpec((None, D, S), lambda b, g, pt: (b, 0, 0)),
                  pl.BlockSpec((None, D, S), lambda b, g, pt: (b, 0, 0))]
        + [page_spec(idx) for idx in range(group)] * 2,
        out_specs=pl.BlockSpec((None, T, D), lambda b, g, pt: (b, 0, 0)),
        scratch_shapes=[pltpu.VMEM((R, D), F32), pltpu.VMEM((R, S), F32)],
    )
    return pl.pallas_call(
        functools.partial(_sattn_kernel, n_groups=n_groups, group=group, n_heads=H, dh=dh, n_new=T),
        grid_spec=grid_spec,
        out_shape=jax.ShapeDtypeStruct((B, T, D), F32),
        compiler_params=_cparams(("parallel", "arbitrary")),
        name="sb_attn_paged",
    )(page_table, qbd, bias, kn, vn, *([ck] * group), *([cv] * group))


def _mlstm_layer(x, n_seq, rows_out, n_pad, chunk, conv_hist, state, wts, ln, alpha):
    w_up, conv_w, conv_b, w_qkv, wg, bg, gn_g, skip, w_down, n_heads = wts
    G, R, _ = x.shape
    I = conv_w.shape[1]
    L = G * R // n_seq
    up = _proj(x, w_up, BF16, tn_cap=2048)
    up_seq = up.reshape(n_seq, L, 2 * I)
    if conv_hist is None:
        conv_src, zero_rows = up_seq, (L - SEQ_BLOCK, L - SEQ_BLOCK + n_pad)
    else:
        nh = conv_hist.shape[1]
        conv_src = jnp.concatenate([up_seq[:, :n_pad - nh, :I], conv_hist.astype(BF16),
                                    up_seq[:, n_pad:, :I]], axis=1)
        zero_rows = None
    xc = _conv_silu(conv_src.reshape(G, R, conv_src.shape[2]), conv_w, conv_b, zero_rows)
    qkv = _qkv_m(xc, up, w_qkv)
    gates = _gates(qkv, wg, bg, n_heads)
    hn, c_new, n_new, m_new = _mlstm(qkv.reshape(n_seq, L, 3 * I), gates.reshape(n_seq, L, GATE_LANES),
                                     n_heads, n_pad, chunk, state)
    x1 = _post_ln((hn.reshape(G, R, I), xc, up, gn_g, skip), x, w_down, ln[0], ln[1], alpha, rows_out,
                  gated=True)
    return x1, up_seq, (c_new, n_new, m_new)


def kernel(x_prompt, x_sample, state_C, state_n, state_m, state_conv, cache_k, cache_v, page_table,
           meta_tokens, w_up, conv_w, conv_b, w_q, w_k, w_v, w_gate, b_gate, gn_g, skip, w_down,
           w_qkv, w_o, sb_bias, mlp_w1, mlp_w2, ln_g, ln_b):
    B, SEQ, D = x_prompt.shape
    BD, T, _ = x_sample.shape
    depth = mlp_w1.shape[0]
    n_meta = meta_tokens.shape[0]
    I = conv_w.shape[2]
    width = conv_w.shape[1]
    HM = b_gate.shape[1] // 2
    HA = sb_bias.shape[1]
    alpha = (2.0 * depth) ** 0.25
    q_scale = -LOG2E * float(D // HA) ** -0.5
    S = SEQ_BLOCK
    assert SEQ % S == 0 and n_meta <= S and T + width - 1 <= SAMPLE_ROWS
    qb = ATTN_Q_BLOCKS
    while (SEQ // S) % qb:
        qb //= 2

    n_pad_p = S - n_meta
    LP = SEQ + S
    xp = jnp.concatenate([x_prompt, jnp.zeros((B, n_pad_p, D), F32),
                          jnp.broadcast_to(meta_tokens.astype(F32)[None], (B, n_meta, D))], axis=1)
    n_pad_s = SAMPLE_ROWS - T
    RS = BD * SAMPLE_ROWS
    xs = jnp.pad(x_sample, ((0, 0), (n_pad_s, 0), (0, 0))).reshape(1, RS, D)

    outs = {name: [] for name in ("p_C", "p_n", "p_m", "p_conv", "p_k", "p_v",
                                  "s_C", "s_n", "s_m", "s_conv", "s_k", "s_v")}
    for i in range(depth):
        j = i // 2
        last = i == depth - 1
        rows_p = SEQ if last else LP
        lng, lnb = ln_g[i], ln_b[i]
        w1, w2 = mlp_w1[i].astype(BF16), mlp_w2[i].astype(BF16)
        if i % 2 == 0:
            wg = jnp.pad(w_gate[j], ((0, 0), (0, GATE_LANES - 2 * HM))).astype(BF16)
            bg = jnp.pad(b_gate[j], (0, GATE_LANES - 2 * HM)).reshape(1, GATE_LANES).astype(F32)
            wts = (w_up[j].astype(BF16), conv_w[j], conv_b[j],
                   jnp.concatenate([w_q[j], w_k[j], w_v[j]], axis=1).astype(BF16), wg, bg,
                   gn_g[j], skip[j], w_down[j].astype(BF16), HM)
            xp1, up_p, (Cp, np_, mp) = _mlstm_layer(xp, B, rows_p, n_pad_p, S, None, None, wts,
                                                    (lng[0], lnb[0]), alpha)
            xs1, up_s, (Cs, ns, ms) = _mlstm_layer(xs, BD, RS, n_pad_s, SAMPLE_ROWS, state_conv[j],
                                                   (state_C[j], state_n[j], state_m[j]), wts,
                                                   (lng[0], lnb[0]), alpha)
            outs["p_C"].append(Cp); outs["p_n"].append(np_); outs["p_m"].append(mp)
            outs["p_conv"].append(up_p[:, SEQ - (width - 1):SEQ, :I].astype(F32))
            outs["s_C"].append(Cs); outs["s_n"].append(ns); outs["s_m"].append(ms)
            hist = jnp.concatenate([state_conv[j], up_s[:, n_pad_s:, :I].astype(F32)], axis=1)
            outs["s_conv"].append(hist[:, T:])
        else:
            wqkv = w_qkv[j].astype(BF16)
            wo = w_o[j].astype(BF16)
            q, k, v, pk, pv = _qkv_a(xp, wqkv, SEQ, n_meta, q_scale)
            o = _attn(q, k, v, sb_bias[j], n_pad_p, qb, 1, SEQ // (qb * S))
            if not last:
                o_meta = _attn(q, k, v, sb_bias[j], n_pad_p, 1, 0, 1)
                o = jnp.concatenate([o, o_meta], axis=1)
            xp1 = _post_ln(o, xp, wo, lng[0], lnb[0], alpha, rows_p, gated=False)
            outs["p_k"].append(pk.reshape(B, n_meta + SEQ, HA, D // HA))
            outs["p_v"].append(pv.reshape(B, n_meta + SEQ, HA, D // HA))
            qkv_s = _proj(xs, wqkv, F32).reshape(BD, SAMPLE_ROWS, 3 * D)[:, n_pad_s:]
            qs, ks, vs = qkv_s[..., :D], qkv_s[..., D:2 * D], qkv_s[..., 2 * D:]
            o_s = _sattn(qs, ks, vs, cache_k[j], cache_v[j], page_table, sb_bias[j], q_scale)
            o_s = jnp.pad(o_s, ((0, 0), (n_pad_s, 0), (0, 0))).astype(BF16).reshape(1, RS, D)
            xs1 = _post_ln(o_s, xs, wo, lng[0], lnb[0], alpha, RS, gated=False)
            outs["s_k"].append(ks.reshape(BD, T, HA, D // HA))
            outs["s_v"].append(vs.reshape(BD, T, HA, D // HA))
        xp = _mlp_ln(xp1, w1, w2, lng[1], lnb[1], alpha, rows_p)
        xs = _mlp_ln(xs1, w1, w2, lng[1], lnb[1], alpha, RS)
    y_prompt = xp[:, :SEQ]
    y_sample = xs.reshape(BD, SAMPLE_ROWS, D)[:, n_pad_s:]
    st = lambda name: jnp.stack(outs[name])
    return (y_prompt, y_sample, st("p_C"), st("p_n"), st("p_m"), st("p_conv"), st("p_k"), st("p_v"),
            st("s_C"), st("s_n"), st("s_m"), st("s_conv"), st("s_k"), st("s_v"))
```

```python
import functools

import jax
import jax.numpy as jnp
from jax import lax
from jax.experimental import pallas as pl
from jax.experimental.pallas import tpu as pltpu

F32 = jnp.float32
BF16 = jnp.bfloat16

LN_EPS = 1e-5
GN_EPS = 1e-6
NEG = -1e30
LOG2E = 1.4426950408889634
SEQ_BLOCK = 128
SAMPLE_ROWS = 16
GATE_LANES = 128
ATTN_Q_BLOCKS = 16
MLSTM_HEAD_GROUP = 4
VMEM_LIMIT = 56 * 1024 * 1024


def _cparams(sem):
    return pltpu.CompilerParams(dimension_semantics=sem, vmem_limit_bytes=VMEM_LIMIT)


def _row_tile(n, cap):
    best = None
    for t in range(16, min(n, cap) + 1, 16):
        if n % t == 0:
            best = t
    assert best is not None, (n, cap)
    return best


def _col_tile(n, cap):
    best = None
    for t in range(128, min(n, cap) + 1, 128):
        if n % t == 0:
            best = t
    assert best is not None, (n, cap)
    return best


def _sigmoid(x):
    return 1.0 / (1.0 + jnp.exp(-x))


def _softplus(x):
    return jnp.maximum(x, 0.0) + jnp.log1p(jnp.exp(-jnp.abs(x)))


def _layer_norm(y, g, b):
    mu = jnp.mean(y, axis=-1, keepdims=True)
    yc = y - mu
    var = jnp.mean(yc * yc, axis=-1, keepdims=True)
    return yc * lax.rsqrt(var + LN_EPS) * g + b


def _proj_kernel(x_ref, w_ref, o_ref):
    o_ref[...] = jnp.dot(x_ref[...].astype(BF16), w_ref[...],
                         preferred_element_type=F32).astype(o_ref.dtype)


def _proj(x, w, out_dtype, tm_cap=1088, tn_cap=1024):
    B, L, K = x.shape
    N = w.shape[1]
    tm, tn = _row_tile(L, tm_cap), _col_tile(N, tn_cap)
    return pl.pallas_call(
        _proj_kernel,
        grid=(B, L // tm, N // tn),
        in_specs=[pl.BlockSpec((None, tm, K), lambda b, i, j: (b, i, 0)),
                  pl.BlockSpec((K, tn), lambda b, i, j: (0, j))],
        out_specs=pl.BlockSpec((None, tm, tn), lambda b, i, j: (b, i, j)),
        out_shape=jax.ShapeDtypeStruct((B, L, N), out_dtype),
        compiler_params=_cparams(("parallel", "parallel", "arbitrary")),
        name="proj",
    )(x, w)


def _qkv_m_kernel(xc_ref, xm_ref, w_ref, o_ref, *, n_qk):
    j = pl.program_id(2)

    @pl.when(j < n_qk)
    def _():
        o_ref[...] = jnp.dot(xc_ref[...], w_ref[...], preferred_element_type=F32).astype(o_ref.dtype)

    @pl.when(j >= n_qk)
    def _():
        o_ref[...] = jnp.dot(xm_ref[...], w_ref[...], preferred_element_type=F32).astype(o_ref.dtype)


def _qkv_m(xc, xm_src, w_qkv):
    B, L, I = xc.shape
    tm, tn = _row_tile(L, 1088), _col_tile(I, 1024)
    return pl.pallas_call(
        functools.partial(_qkv_m_kernel, n_qk=2 * I // tn),
        grid=(B, L // tm, 3 * I // tn),
        in_specs=[pl.BlockSpec((None, tm, I), lambda b, i, j: (b, i, 0)),
                  pl.BlockSpec((None, tm, I), lambda b, i, j: (b, i, 0)),
                  pl.BlockSpec((I, tn), lambda b, i, j: (0, j))],
        out_specs=pl.BlockSpec((None, tm, tn), lambda b, i, j: (b, i, j)),
        out_shape=jax.ShapeDtypeStruct((B, L, 3 * I), BF16),
        compiler_params=_cparams(("parallel", "parallel", "arbitrary")),
        name="qkv_m",
    )(xc, xm_src, w_qkv)


def _gates_kernel(x_ref, w_ref, b_ref, o_ref, *, n_heads):
    g = jnp.dot(x_ref[...], w_ref[...], preferred_element_type=F32) + b_ref[...]
    lane = lax.broadcasted_iota(jnp.int32, g.shape, 1)
    o_ref[...] = jnp.where(lane < n_heads, g, -_softplus(-g))


def _gates(qkv, wg, bg, n_heads):
    B, L, K = qkv.shape
    tm = _row_tile(L, 544)
    return pl.pallas_call(
        functools.partial(_gates_kernel, n_heads=n_heads),
        grid=(B, L // tm),
        in_specs=[pl.BlockSpec((None, tm, K), lambda b, i: (b, i, 0)),
                  pl.BlockSpec((K, GATE_LANES), lambda b, i: (0, 0)),
                  pl.BlockSpec((1, GATE_LANES), lambda b, i: (0, 0))],
        out_specs=pl.BlockSpec((None, tm, GATE_LANES), lambda b, i: (b, i, 0)),
        out_shape=jax.ShapeDtypeStruct((B, L, GATE_LANES), F32),
        compiler_params=_cparams(("parallel", "parallel")),
        name="gates",
    )(qkv, wg, bg)


def _conv_kernel(x_ref, w_ref, b_ref, o_ref, *, width, zero_rows):
    x = x_ref[...].astype(F32)
    if zero_rows is not None:
        row = lax.broadcasted_iota(jnp.int32, x.shape, 0)
        x = jnp.where((row >= zero_rows[0]) & (row < zero_rows[1]), 0.0, x)
    acc = b_ref[...] + w_ref[width - 1:width, :] * x
    for j in range(width - 1):
        acc = acc + w_ref[j:j + 1, :] * pltpu.roll(x, width - 1 - j, axis=0)
    o_ref[...] = (acc * _sigmoid(acc)).astype(o_ref.dtype)


def _conv_silu(x_src, conv_w, conv_b, zero_rows):
    B, L, _ = x_src.shape
    width, I = conv_w.shape
    tc = _col_tile(I, 256)
    return pl.pallas_call(
        functools.partial(_conv_kernel, width=width, zero_rows=zero_rows),
        grid=(B, I // tc),
        in_specs=[pl.BlockSpec((None, L, tc), lambda b, j: (b, 0, j)),
                  pl.BlockSpec((width, tc), lambda b, j: (0, j)),
                  pl.BlockSpec((1, tc), lambda b, j: (0, j))],
        out_specs=pl.BlockSpec((None, L, tc), lambda b, j: (b, 0, j)),
        out_shape=jax.ShapeDtypeStruct((B, L, I), BF16),
        compiler_params=_cparams(("parallel", "parallel")),
        name="conv_silu",
    )(x_src, conv_w, conv_b.reshape(1, I))


def _mlstm_kernel(*refs, n_heads, n_chunks, n_pad, scale, has_state):
    if has_state:
        (q_ref, k_ref, v_ref, g_ref, gt_ref, c0_ref, n0_ref, m0_ref,
         h_ref, c_out, n_out, m_out, c_sc, n_sc, m_sc) = refs
    else:
        (q_ref, k_ref, v_ref, g_ref, gt_ref,
         h_ref, c_out, n_out, m_out, c_sc, n_sc, m_sc) = refs
    c = pl.program_id(1)
    L = q_ref.shape[0]
    dh = q_ref.shape[1] // n_heads

    @pl.when(c == 0)
    def _():
        if has_state:
            c_sc[...] = c0_ref[...]
            n_sc[...] = n0_ref[...]
            m_sc[...] = m0_ref[...]
        else:
            c_sc[...] = jnp.zeros_like(c_sc)
            n_sc[...] = jnp.zeros_like(n_sc)
            m_sc[...] = jnp.zeros_like(m_sc)

    n_pad_c = jnp.where(c == 0, n_pad, 0)
    g = g_ref[...]
    row_g = lax.broadcasted_iota(jnp.int32, g.shape, 0)
    lane_g = lax.broadcasted_iota(jnp.int32, g.shape, 1)
    g = jnp.where(row_g < n_pad_c, jnp.where(lane_g < n_heads, NEG, 0.0), g)
    gt = gt_ref[...]
    sub_t = lax.broadcasted_iota(jnp.int32, gt.shape, 0)
    lane_t = lax.broadcasted_iota(jnp.int32, gt.shape, 1)
    gt = jnp.where(lane_t < n_pad_c, jnp.where(sub_t < n_heads, NEG, 0.0), gt)

    r = lax.broadcasted_iota(jnp.int32, (L, L), 0)
    s = lax.broadcasted_iota(jnp.int32, (L, L), 1)
    causal = s <= r
    tri = causal.astype(F32)
    cum_col = jnp.dot(tri, jnp.where(lane_g >= n_heads, g, 0.0),
                      preferred_element_type=F32, precision=lax.Precision.HIGHEST)
    cum_row = lax.dot_general(jnp.where(sub_t >= n_heads, gt, 0.0), tri, (((1,), (1,)), ((), ())),
                              preferred_element_type=F32, precision=lax.Precision.HIGHEST)

    def head_group(heads):
        cols = {hd: slice(hd * dh, (hd + 1) * dh) for hd in heads}
        b_col = {hd: cum_col[:, n_heads + hd:n_heads + hd + 1] for hd in heads}
        ig_col = {hd: g[:, hd:hd + 1] for hd in heads}
        b_row = {hd: cum_row[n_heads + hd:n_heads + hd + 1, :] for hd in heads}
        ig_row = {hd: gt[hd:hd + 1, :] for hd in heads}
        b_end = {hd: jnp.sum(gt[n_heads + hd:n_heads + hd + 1, :], axis=1, keepdims=True) for hd in heads}
        m = {hd: m_sc[hd][:, 0:1] for hd in heads}
        q = {h: q_ref[:, cols[h]] for h in heads}
        k = {h: k_ref[:, cols[h]] for h in heads}
        v = {h: v_ref[:, cols[h]] for h in heads}

        nt = (((1,), (1,)), ((), ()))
        qk = {h: lax.dot_general(q[h], k[h], nt, preferred_element_type=F32) for h in heads}
        qc = {h: jnp.dot(q[h], c_sc[h].astype(BF16), preferred_element_type=F32) for h in heads}
        d = {h: jnp.where(causal, b_col[h] - b_row[h] + ig_row[h], NEG) for h in heads}
        m_inter = {h: b_col[h] + m[h] for h in heads}
        m_t = {h: jnp.maximum(m_inter[h], jnp.max(d[h], axis=1, keepdims=True)) for h in heads}
        sw = {h: qk[h] * (scale * jnp.exp(d[h] - m_t[h])) for h in heads}
        sc = {h: jnp.exp(m_inter[h] - m_t[h]) for h in heads}
        num = {h: jnp.dot(sw[h].astype(BF16), v[h], preferred_element_type=F32) + sc[h] * qc[h] for h in heads}
        den = {h: jnp.sum(sw[h], axis=1, keepdims=True)
               + sc[h] * jnp.sum(q[h].astype(F32) * n_sc[h], axis=1, keepdims=True) for h in heads}
        hh = {h: num[h] * (1.0 / jnp.maximum(jnp.abs(den[h]), jnp.exp(-m_t[h]))) for h in heads}
        mu = {h: jnp.mean(hh[h], axis=1, keepdims=True) for h in heads}
        hc = {h: hh[h] - mu[h] for h in heads}
        var = {h: jnp.mean(hc[h] * hc[h], axis=1, keepdims=True) for h in heads}
        for h in heads:
            h_ref[:, cols[h]] = (hc[h] * lax.rsqrt(var[h] + GN_EPS)).astype(h_ref.dtype)

        g_col = {h: b_end[h] - b_col[h] + ig_col[h] for h in heads}
        g_row = {h: b_end[h] - b_row[h] + ig_row[h] for h in heads}
        m_new = {h: jnp.maximum(b_end[h] + m[h], jnp.max(g_row[h], axis=1, keepdims=True)) for h in heads}
        w_col = {h: scale * jnp.exp(g_col[h] - m_new[h]) for h in heads}
        decay = {h: jnp.exp(b_end[h] + m[h] - m_new[h]) for h in heads}
        vw = {h: (v[h].astype(F32) * w_col[h]).astype(BF16) for h in heads}
        tn = (((0,), (0,)), ((), ()))
        kv = {h: lax.dot_general(k[h], vw[h], tn, preferred_element_type=F32) for h in heads}
        for h in heads:
            c_sc[h] = decay[h] * c_sc[h] + kv[h]
            n_sc[h] = decay[h] * n_sc[h] + jnp.sum(k[h].astype(F32) * w_col[h], axis=0, keepdims=True)
            m_sc[h] = jnp.broadcast_to(m_new[h], m_sc.shape[1:])

    for h0 in range(0, n_heads, MLSTM_HEAD_GROUP):
        head_group(range(h0, min(h0 + MLSTM_HEAD_GROUP, n_heads)))

    @pl.when(c == n_chunks - 1)
    def _():
        c_out[...] = c_sc[...]
        n_out[...] = n_sc[...]
        m_out[...] = m_sc[...]


def _mlstm(qkv, gates, n_heads, n_pad, chunk, state=None):
    B, L, I3 = qkv.shape
    I = I3 // 3
    dh = I // n_heads
    nc = L // chunk
    assert 2 * n_heads <= 8
    gates_t = jnp.swapaxes(gates[:, :, :8], 1, 2)
    has_state = state is not None

    def blk(c):
        return (c + nc - 1) % nc

    state_specs = [pl.BlockSpec((None, n_heads, dh, dh), lambda b, c: (b, 0, 0, 0)),
                   pl.BlockSpec((None, n_heads, 1, dh), lambda b, c: (b, 0, 0, 0)),
                   pl.BlockSpec((None, n_heads, 1, GATE_LANES), lambda b, c: (b, 0, 0, 0))]
    in_specs = [pl.BlockSpec((None, chunk, I), lambda b, c: (b, blk(c), 0)),
                pl.BlockSpec((None, chunk, I), lambda b, c: (b, blk(c), 1)),
                pl.BlockSpec((None, chunk, I), lambda b, c: (b, blk(c), 2)),
                pl.BlockSpec((None, chunk, GATE_LANES), lambda b, c: (b, blk(c), 0)),
                pl.BlockSpec((None, 8, chunk), lambda b, c: (b, 0, blk(c)))]
    args = [qkv, qkv, qkv, gates, gates_t]
    if has_state:
        c0, n0, m0 = state
        in_specs += state_specs
        args += [c0, n0.reshape(B, n_heads, 1, dh),
                 jnp.broadcast_to(m0.reshape(B, n_heads, 1, 1), (B, n_heads, 1, GATE_LANES))]
    out_shape = (jax.ShapeDtypeStruct((B, L, I), BF16),
                 jax.ShapeDtypeStruct((B, n_heads, dh, dh), F32),
                 jax.ShapeDtypeStruct((B, n_heads, 1, dh), F32),
                 jax.ShapeDtypeStruct((B, n_heads, 1, GATE_LANES), F32))
    out_specs = [pl.BlockSpec((None, chunk, I), lambda b, c: (b, blk(c), 0))] + state_specs
    hn, c_new, n_new, m_new = pl.pallas_call(
        functools.partial(_mlstm_kernel, n_heads=n_heads, n_chunks=nc, n_pad=n_pad,
                          scale=float(dh) ** -0.5, has_state=has_state),
        grid=(B, nc),
        in_specs=in_specs,
        out_specs=out_specs,
        out_shape=out_shape,
        scratch_shapes=[pltpu.VMEM((n_heads, dh, dh), F32), pltpu.VMEM((n_heads, 1, dh), F32),
                        pltpu.VMEM((n_heads, 1, GATE_LANES), F32)],
        compiler_params=_cparams(("parallel", "arbitrary")),
        name="mlstm",
    )(*args)
    return hn, c_new, n_new[:, :, 0, :], m_new[:, :, 0, 0]


def _post_ln_kernel(*refs, gated, alpha):
    if gated:
        hn_ref, xc_ref, z_ref, gn_ref, sk_ref, res_ref, w_ref, g_ref, b_ref, o_ref = refs
        z = z_ref[...].astype(F32)
        hcomb = hn_ref[...].astype(F32) * gn_ref[...] + sk_ref[...] * xc_ref[...].astype(F32)
        lhs = (hcomb * (z * _sigmoid(z))).astype(BF16)
    else:
        lhs_ref, res_ref, w_ref, g_ref, b_ref, o_ref = refs
        lhs = lhs_ref[...]
    mix = jnp.dot(lhs, w_ref[...], preferred_element_type=F32)
    y = alpha * res_ref[...] + mix
    o_ref[...] = _layer_norm(y, g_ref[...], b_ref[...])


def _post_ln(lhs_args, res, w, ln_g, ln_b, alpha, rows, gated):
    B = res.shape[0]
    K, D = w.shape
    tm = _row_tile(rows, 544)
    row = lambda b, i: (b, i, 0)
    const = lambda b, i: (0, 0)
    if gated:
        hn, xc, z_src, gn_g, skip = lhs_args
        nz = z_src.shape[2] // K - 1
        in_specs = [pl.BlockSpec((None, tm, K), row),
                    pl.BlockSpec((None, tm, K), row),
                    pl.BlockSpec((None, tm, K), lambda b, i: (b, i, nz)),
                    pl.BlockSpec((1, K), const),
                    pl.BlockSpec((1, K), const)]
        args = [hn, xc, z_src, gn_g.reshape(1, K), skip.reshape(1, K)]
    else:
        in_specs = [pl.BlockSpec((None, tm, K), row)]
        args = [lhs_args]
    in_specs += [pl.BlockSpec((None, tm, D), row),
                 pl.BlockSpec((K, D), const),
                 pl.BlockSpec((1, D), const),
                 pl.BlockSpec((1, D), const)]
    args += [res, w, ln_g.reshape(1, D), ln_b.reshape(1, D)]
    return pl.pallas_call(
        functools.partial(_post_ln_kernel, gated=gated, alpha=alpha),
        grid=(B, rows // tm),
        in_specs=in_specs,
        out_specs=pl.BlockSpec((None, tm, D), row),
        out_shape=jax.ShapeDtypeStruct((B, rows, D), F32),
        compiler_params=_cparams(("parallel", "parallel")),
        name="post_ln_gated" if gated else "post_ln",
    )(*args)


def _mlp_kernel(*refs, alpha, n_chunks, mixed):
    if mixed:
        x_ref, lhs_ref, wm_ref, g1_ref, b1_ref, w1_ref, w2_ref, g_ref, b_ref, o_ref = refs
        mixer = jnp.dot(lhs_ref[...], wm_ref[...], preferred_element_type=F32)
        x = _layer_norm(alpha * x_ref[...] + mixer, g1_ref[...], b1_ref[...])
    else:
        x_ref, w1_ref, w2_ref, g_ref, b_ref, o_ref = refs
        x = x_ref[...]
    xb = x.astype(BF16)
    tf = w1_ref.shape[1] // n_chunks
    mix = None
    for c in range(n_chunks):
        hid = jnp.maximum(jnp.dot(xb, w1_ref[:, c * tf:(c + 1) * tf], preferred_element_type=F32), 0.0)
        part = jnp.dot((hid * hid).astype(BF16), w2_ref[c * tf:(c + 1) * tf, :], preferred_element_type=F32)
        mix = part if mix is None else mix + part
    o_ref[...] = _layer_norm(alpha * x + mix, g_ref[...], b_ref[...])


def _mlp_ln(x, w1, w2, ln_g, ln_b, alpha, rows, mix=None):
    B, _, D = x.shape
    F = w1.shape[1]
    tm = _row_tile(rows, 544)
    once = pl.Buffered(1)
    row = lambda b, i: (b, i, 0)
    const = lambda b, i: (0, 0)
    vec = pl.BlockSpec((1, D), const)
    in_specs, args = [pl.BlockSpec((None, tm, D), row)], [x]
    if mix is not None:
        lhs, wm, g1, b1 = mix
        K = wm.shape[0]
        in_specs += [pl.BlockSpec((None, tm, K), row), pl.BlockSpec((K, D), const, pipeline_mode=once), vec, vec]
        args += [lhs, wm, g1.reshape(1, D), b1.reshape(1, D)]
    in_specs += [pl.BlockSpec((D, F), const, pipeline_mode=once),
                 pl.BlockSpec((F, D), const, pipeline_mode=once), vec, vec]
    args += [w1, w2, ln_g.reshape(1, D), ln_b.reshape(1, D)]
    return pl.pallas_call(
        functools.partial(_mlp_kernel, alpha=alpha, n_chunks=F // _col_tile(F, 2048), mixed=mix is not None),
        grid=(B, rows // tm),
        in_specs=in_specs,
        out_specs=pl.BlockSpec((None, tm, D), row),
        out_shape=jax.ShapeDtypeStruct((B, rows, D), F32),
        compiler_params=_cparams(("parallel", "parallel")),
        name="mix_mlp_ln" if mix is not None else "mlp_ln",
    )(*args)


def _qkv_a_kernel(x_ref, wq_ref, wk_ref, wv_ref, q_ref, k_ref, v_ref, pk_ref, pv_ref, *,
                  seq, n_meta, q_scale):
    x = x_ref[...].astype(BF16)
    L = x.shape[0]
    q = jnp.dot(x, wq_ref[...], preferred_element_type=F32)
    q_ref[...] = (q * q_scale).astype(q_ref.dtype)
    for w_ref, o_ref, p_ref in ((wk_ref, k_ref, pk_ref), (wv_ref, v_ref, pv_ref)):
        y = jnp.dot(x, w_ref[...], preferred_element_type=F32)
        o_ref[...] = y.astype(o_ref.dtype)
        p_ref[0:n_meta, :] = y[L - n_meta:L, :]
        p_ref[n_meta:n_meta + seq, :] = y[0:seq, :]


def _qkv_a(x, w_qkv, seq, n_meta, q_scale):
    B, L, D = x.shape
    tn = _col_tile(D, 256)
    nj = D // tn
    xs = pl.BlockSpec((None, L, D), lambda b, j: (b, 0, 0))
    os = pl.BlockSpec((None, L, tn), lambda b, j: (b, 0, j))
    ps = pl.BlockSpec((None, n_meta + seq, tn), lambda b, j: (b, 0, j))
    return pl.pallas_call(
        functools.partial(_qkv_a_kernel, seq=seq, n_meta=n_meta, q_scale=q_scale),
        grid=(B, nj),
        in_specs=[xs,
                  pl.BlockSpec((D, tn), lambda b, j: (0, j)),
                  pl.BlockSpec((D, tn), lambda b, j: (0, nj + j)),
                  pl.BlockSpec((D, tn), lambda b, j: (0, 2 * nj + j))],
        out_specs=(os, os, os, ps, ps),
        out_shape=(jax.ShapeDtypeStruct((B, L, D), BF16),) * 3
        + (jax.ShapeDtypeStruct((B, n_meta + seq, D), F32),) * 2,
        compiler_params=_cparams(("parallel", "arbitrary")),
        name="qkv_a",
    )(x, w_qkv, w_qkv, w_qkv)


def _upper_and_ones(n):
    j = lax.broadcasted_iota(jnp.int32, (n, 2 * n), 0)
    s = lax.broadcasted_iota(jnp.int32, (n, 2 * n), 1)
    return ((j > s) | (s >= n)).astype(BF16)


def _strict_upper(n):
    j = lax.broadcasted_iota(jnp.int32, (n, n), 0)
    s = lax.broadcasted_iota(jnp.int32, (n, n), 1)
    return (j > s).astype(BF16)


def _keep(x, ok):
    if ok is None:
        return x
    m = ok.shape[0]
    if m == x.shape[0]:
        return jnp.where(ok, x, 0.0)
    return jnp.concatenate([jnp.where(ok, x[:m], 0.0), x[m:]], axis=0)


def _sb_weights(zns, valid, carry, su_ones, su_pair):
    S = zns[0].shape[1]
    valid = [None] * len(zns) if valid is None else valid
    lks = []
    for zn, ok in zip(zns, valid):
        lk = jnp.minimum(zn, 0.0) - LOG2E * jnp.log(1.0 + jnp.exp2(-jnp.abs(zn)))
        lks.append(_keep(lk, ok))
    n_pairs = len(zns) // 2
    afters, totals = [], []
    for p in range(n_pairs):
        late, early = lks[2 * p], lks[2 * p + 1]
        res = jnp.dot(jnp.concatenate([early, late], axis=1).astype(BF16), su_pair, preferred_element_type=F32)
        afters += [res[:, S:], res[:, :S]]
        totals += [None, res[:, 0:1] + early[:, 0:1]]
    for lk in lks[2 * n_pairs:]:
        res = jnp.dot(lk.astype(BF16), su_ones, preferred_element_type=F32)
        afters.append(res[:, :S])
        totals.append(res[:, S:])
    out = []
    for zn, lk, after, total, ok in zip(zns, lks, afters, totals, valid):
        a = jnp.exp2((lk - zn) + (carry + after))
        out.append(_keep(a, ok))
        if total is not None:
            carry = carry + total
    return out, carry


def _attn_kernel(q_ref, k_ref, v_ref, bias_ref, o_ref, carry_sc, acc_sc, *,
                 qb, first_block, n_blocks, n_pad, dh):
    S = SEQ_BLOCK
    lb0 = first_block + pl.program_id(2) * qb
    first = lax.broadcasted_iota(jnp.int32, (S, 2 * dh), 1) < dh
    zero = jnp.zeros((S, 2 * dh), BF16)

    def tall(x):
        return jnp.concatenate([jnp.where(first, x, zero), jnp.where(first, zero, x)], axis=0)

    q_tall = jnp.concatenate([tall(q_ref[r * S:(r + 1) * S, :]) for r in range(qb)], axis=0)
    su_ones, su_pair = _upper_and_ones(S), _strict_upper(2 * S)
    carry_sc[...] = jnp.zeros_like(carry_sc)
    acc_sc[...] = jnp.zeros_like(acc_sc)

    def key_tile(j, n, r0, diag, pad):
        nr = qb - r0
        rows = slice(r0 * 2 * S, qb * 2 * S)
        start = ((j - n + n_blocks) % n_blocks) * S
        if not isinstance(j, int):
            start = pl.multiple_of(start, S)
        z_all = lax.dot_general(q_tall[rows], k_ref[pl.ds(start, n * S), :], (((1,), (1,)), ((), ())),
                                preferred_element_type=F32)
        bias = bias_ref[rows, :]
        if pad:
            bias = jnp.where(lax.broadcasted_iota(jnp.int32, bias.shape, 1) < n_pad, -NEG, bias)
        v_all = v_ref[pl.ds(start, n * S), :]
        zs = [z_all[:, c * S:(c + 1) * S] + bias for c in reversed(range(n))]
        vts = [tall(v_all[c * S:(c + 1) * S, :]) for c in reversed(range(n))]
        valid = None
        if diag:
            riota = lax.broadcasted_iota(jnp.int32, (n * 2 * S, S), 0)
            inside = lax.broadcasted_iota(jnp.int32, (n * 2 * S, S), 1) < (riota & (S - 1))
            valid = [(riota >= (c + 1) * 2 * S) | ((riota >= c * 2 * S) & inside) for c in reversed(range(n))]
        a_list, carry = _sb_weights(zs, valid, carry_sc[rows, :], su_ones, su_pair)
        carry_sc[rows, :] = carry

        def wide(a):
            a = a.astype(BF16)
            return jnp.concatenate(
                [jnp.concatenate([a[r * 2 * S:r * 2 * S + S], a[r * 2 * S + S:(r + 1) * 2 * S]], axis=1)
                 for r in range(nr)], axis=0)

        a_wide = jnp.concatenate([wide(a) for a in a_list], axis=1)
        acc_sc[r0 * S:qb * S, :] += jnp.dot(a_wide, jnp.concatenate(vts, axis=0), preferred_element_type=F32)

    nd = 2 if qb % 2 == 0 else 1
    for t in range(qb // nd):
        r0 = qb - nd * (t + 1)
        key_tile(lb0 + r0 + nd - 1, nd, r0, diag=True, pad=first_block == 0)
    if first_block > 0:
        n = 4 if qb % 4 == 0 else 2 if qb % 2 == 0 else 1

        def full(t, _):
            key_tile(lb0 - 1 - n * t, n, 0, diag=False, pad=False)
            return 0

        lax.fori_loop(0, (lb0 - 1) // n, full, 0)
        key_tile(0, 1, 0, diag=False, pad=True)
    o_ref[...] = acc_sc[...].astype(o_ref.dtype)


def _attn(q, k, v, sb_bias, n_pad, qb, first_block, n_tiles):
    B, L, D = q.shape
    H = sb_bias.shape[0]
    dh = D // H
    assert 2 * dh == SEQ_BLOCK
    S = SEQ_BLOCK
    R = qb * S
    nb = L // S
    bias = jnp.broadcast_to((-LOG2E * sb_bias.astype(F32)).reshape(H // 2, 1, 2, 1, 1), (H // 2, qb, 2, S, S))
    bias = bias.reshape(H // 2, 2 * R, S)
    q_index = lambda b, p, i: (b, (first_block + nb - 1) % nb // qb + i, p)
    return pl.pallas_call(
        functools.partial(_attn_kernel, qb=qb, first_block=first_block, n_blocks=nb, n_pad=n_pad, dh=dh),
        grid=(B, H // 2, n_tiles),
        in_specs=[pl.BlockSpec((None, R, 2 * dh), q_index),
                  pl.BlockSpec((None, L, 2 * dh), lambda b, p, i: (b, 0, p)),
                  pl.BlockSpec((None, L, 2 * dh), lambda b, p, i: (b, 0, p)),
                  pl.BlockSpec((None, 2 * R, S), lambda b, p, i: (p, 0, 0))],
        out_specs=pl.BlockSpec((None, R, 2 * dh), lambda b, p, i: (b, i, p)),
        out_shape=jax.ShapeDtypeStruct((B, n_tiles * R, D), BF16),
        scratch_shapes=[pltpu.VMEM((2 * R, S), F32), pltpu.VMEM((R, 2 * dh), F32)],
        compiler_params=_cparams(("parallel", "parallel", "arbitrary")),
        name="sb_attn",
    )(q, k, v, bias)


def _sattn_kernel(pt_ref, qbd_ref, bias_ref, kn_ref, vn_ref, *refs, n_groups, group, n_heads, dh, n_new):
    del pt_ref
    k_refs = refs[:group]
    v_refs = refs[group:2 * group]
    o_ref, acc_ref, carry_ref = refs[2 * group:]
    g = pl.program_id(1)
    qbd = qbd_ref[...]
    bias = bias_ref[...]
    R, S = bias.shape
    su_ones, su_pair = _upper_and_ones(S), _strict_upper(2 * S)

    def blocks(kts, vts, valid):
        zs = [jnp.dot(qbd, kt, preferred_element_type=F32) + bias for kt in kts]
        a_list, carry = _sb_weights(zs, valid, carry_ref[...], su_ones, su_pair)
        carry_ref[...] = carry
        part = None
        for a, vt in zip(a_list, vts):
            av = lax.dot_general(a.astype(BF16), vt, (((1,), (1,)), ((), ())), preferred_element_type=F32)
            part = av if part is None else part + av
        acc_ref[...] += part

    @pl.when(g == 0)
    def _():
        acc_ref[...] = jnp.zeros_like(acc_ref)
        carry_ref[...] = jnp.zeros_like(carry_ref)
        t = lax.broadcasted_iota(jnp.int32, (R, S), 0) // n_heads
        s = lax.broadcasted_iota(jnp.int32, (R, S), 1)
        blocks([kn_ref[...]], [vn_ref[...]], [s < t])

    blocks([r[...].astype(BF16) for r in k_refs], [r[...].astype(BF16) for r in v_refs], None)

    @pl.when(g == n_groups - 1)
    def _():
        acc = acc_ref[...]
        row_h = lax.broadcasted_iota(jnp.int32, acc.shape, 0) % n_heads
        col_h = lax.broadcasted_iota(jnp.int32, acc.shape, 1) // dh
        own = jnp.where(row_h == col_h, acc, 0.0)
        o_ref[...] = jnp.sum(own.reshape(n_new, n_heads, acc.shape[1]), axis=1)


def _sattn(q, k_new, v_new, cache_k, cache_v, page_table, sb_bias, q_scale):
    B, T, D = q.shape
    H = sb_bias.shape[0]
    dh = D // H
    n_pool, S = cache_k.shape[0], cache_k.shape[1]
    n_pages = page_table.shape[1]
    group = max(g for g in (8, 4, 2, 1) if n_pages % g == 0)
    n_groups = n_pages // group
    R = T * H
    head_of_col = jnp.arange(D, dtype=jnp.int32) // dh
    own = head_of_col[None, None, :] == jnp.arange(H, dtype=jnp.int32)[None, :, None]
    qbd = jnp.where(own[:, None], (q * q_scale)[:, :, None, :], 0.0).reshape(B, R, D).astype(BF16)
    bias = jnp.broadcast_to(jnp.tile(-LOG2E * sb_bias.astype(F32), T)[:, None], (R, S))
    pad = ((0, 0), (0, S - T), (0, 0))
    kn = jnp.swapaxes(jnp.pad(k_new, pad), 1, 2).astype(BF16)
    vn = jnp.swapaxes(jnp.pad(v_new, pad), 1, 2).astype(BF16)
    ck = jnp.transpose(cache_k, (0, 2, 3, 1)).reshape(n_pool, D, S)
    cv = jnp.transpose(cache_v, (0, 2, 3, 1)).reshape(n_pool, D, S)

    def page_spec(idx):
        return pl.BlockSpec((None, D, S),
                            lambda b, g, pt: (pt[b, n_pages - 1 - (g * group + idx)], 0, 0))

    grid_spec = pltpu.PrefetchScalarGridSpec(
        num_scalar_prefetch=1,
        grid=(B, n_groups),
        in_specs=[pl.BlockSpec((None, R, D), lambda b, g, pt: (b, 0, 0)),
                  pl.BlockSpec((R, S), lambda b, g, pt: (0, 0)),
                  pl.BlockSpec((None, D, S), lambda b, g, pt: (b, 0, 0)),
                  pl.BlockSpec((None, D, S), lambda b, g, pt: (b, 0, 0))]
        + [page_spec(idx) for idx in range(group)] * 2,
        out_specs=pl.BlockSpec((None, T, D), lambda b, g, pt: (b, 0, 0)),
        scratch_shapes=[pltpu.VMEM((R, D), F32), pltpu.VMEM((R, S), F32)],
    )
    return pl.pallas_call(
        functools.partial(_sattn_kernel, n_groups=n_groups, group=group, n_heads=H, dh=dh, n_new=T),
        grid_spec=grid_spec,
        out_shape=jax.ShapeDtypeStruct((B, T, D), F32),
        compiler_params=_cparams(("parallel", "arbitrary")),
        name="sb_attn_paged",
    )(page_table, qbd, bias, kn, vn, *([ck] * group), *([cv] * group))


def _mlstm_layer(x, n_seq, rows_out, n_pad, chunk, conv_hist, state, wts, ln, alpha, pad_is_zero=False):
    w_up, conv_w, conv_b, w_qkv, wg, bg, gn_g, skip, w_down, n_heads = wts
    G, R, _ = x.shape
    I = conv_w.shape[1]
    L = G * R // n_seq
    up = _proj(x, w_up, BF16, tn_cap=2048)
    up_seq = up.reshape(n_seq, L, 2 * I)
    if conv_hist is None:
        conv_src, zero_rows = up_seq, None if pad_is_zero else (L - SEQ_BLOCK, L - SEQ_BLOCK + n_pad)
    else:
        nh = conv_hist.shape[1]
        conv_src = jnp.concatenate([up_seq[:, :n_pad - nh, :I], conv_hist.astype(BF16),
                                    up_seq[:, n_pad:, :I]], axis=1)
        zero_rows = None
    xc = _conv_silu(conv_src.reshape(G, R, conv_src.shape[2]), conv_w, conv_b, zero_rows)
    qkv = _qkv_m(xc, up, w_qkv)
    gates = _gates(qkv, wg, bg, n_heads)
    hn, c_new, n_new, m_new = _mlstm(qkv.reshape(n_seq, L, 3 * I), gates.reshape(n_seq, L, GATE_LANES),
                                     n_heads, n_pad, chunk, state)
    x1 = _post_ln((hn.reshape(G, R, I), xc, up, gn_g, skip), x, w_down, ln[0], ln[1], alpha, rows_out,
                  gated=True)
    return x1, up_seq, (c_new, n_new, m_new)


def kernel(x_prompt, x_sample, state_C, state_n, state_m, state_conv, cache_k, cache_v, page_table,
           meta_tokens, w_up, conv_w, conv_b, w_q, w_k, w_v, w_gate, b_gate, gn_g, skip, w_down,
           w_qkv, w_o, sb_bias, mlp_w1, mlp_w2, ln_g, ln_b):
    B, SEQ, D = x_prompt.shape
    BD, T, _ = x_sample.shape
    depth = mlp_w1.shape[0]
    n_meta = meta_tokens.shape[0]
    I = conv_w.shape[2]
    width = conv_w.shape[1]
    HM = b_gate.shape[1] // 2
    HA = sb_bias.shape[1]
    alpha = (2.0 * depth) ** 0.25
    q_scale = -LOG2E * float(D // HA) ** -0.5
    S = SEQ_BLOCK
    assert SEQ % S == 0 and n_meta <= S and T + width - 1 <= SAMPLE_ROWS
    qb = ATTN_Q_BLOCKS
    while (SEQ // S) % qb:
        qb //= 2

    n_pad_p = S - n_meta
    LP = SEQ + S
    xp = jnp.concatenate([x_prompt, jnp.zeros((B, n_pad_p, D), F32),
                          jnp.broadcast_to(meta_tokens.astype(F32)[None], (B, n_meta, D))], axis=1)
    n_pad_s = SAMPLE_ROWS - T
    RS = BD * SAMPLE_ROWS
    xs = jnp.pad(x_sample, ((0, 0), (n_pad_s, 0), (0, 0))).reshape(1, RS, D)

    outs = {name: [] for name in ("p_C", "p_n", "p_m", "p_conv", "p_k", "p_v",
                                  "s_C", "s_n", "s_m", "s_conv", "s_k", "s_v")}
    for i in range(depth):
        j = i // 2
        last = i == depth - 1
        rows_p = SEQ if last else LP
        lng, lnb = ln_g[i], ln_b[i]
        w1, w2 = mlp_w1[i].astype(BF16), mlp_w2[i].astype(BF16)
        if i % 2 == 0:
            wg = jnp.pad(w_gate[j], ((0, 0), (0, GATE_LANES - 2 * HM))).astype(BF16)
            bg = jnp.pad(b_gate[j], (0, GATE_LANES - 2 * HM)).reshape(1, GATE_LANES).astype(F32)
            wts = (w_up[j].astype(BF16), conv_w[j], conv_b[j],
                   jnp.concatenate([w_q[j], w_k[j], w_v[j]], axis=1).astype(BF16), wg, bg,
                   gn_g[j], skip[j], w_down[j].astype(BF16), HM)
            xp1, up_p, (Cp, np_, mp) = _mlstm_layer(xp, B, rows_p, n_pad_p, S, None, None, wts,
                                                    (lng[0], lnb[0]), alpha, pad_is_zero=i == 0)
            xs1, up_s, (Cs, ns, ms) = _mlstm_layer(xs, BD, RS, n_pad_s, SAMPLE_ROWS, state_conv[j],
                                                   (state_C[j], state_n[j], state_m[j]), wts,
                                                   (lng[0], lnb[0]), alpha)
            outs["p_C"].append(Cp); outs["p_n"].append(np_); outs["p_m"].append(mp)
            outs["p_conv"].append(up_p[:, SEQ - (width - 1):SEQ, :I].astype(F32))
            outs["s_C"].append(Cs); outs["s_n"].append(ns); outs["s_m"].append(ms)
            hist = jnp.concatenate([state_conv[j], up_s[:, n_pad_s:, :I].astype(F32)], axis=1)
            outs["s_conv"].append(hist[:, T:])
            mix_p = mix_s = None
        else:
            wqkv = w_qkv[j].astype(BF16)
            wo = w_o[j].astype(BF16)
            q, k, v, pk, pv = _qkv_a(xp, wqkv, SEQ, n_meta, q_scale)
            o = _attn(q, k, v, sb_bias[j], n_pad_p, qb, 1, SEQ // (qb * S))
            if not last:
                o_meta = _attn(q, k, v, sb_bias[j], n_pad_p, 1, 0, 1)
                o = jnp.concatenate([o, o_meta], axis=1)
            xp1, mix_p = xp, (o, wo, lng[0], lnb[0])
            outs["p_k"].append(pk.reshape(B, n_meta + SEQ, HA, D // HA))
            outs["p_v"].append(pv.reshape(B, n_meta + SEQ, HA, D // HA))
            qkv_s = _proj(xs, wqkv, F32).reshape(BD, SAMPLE_ROWS, 3 * D)[:, n_pad_s:]
            qs, ks, vs = qkv_s[..., :D], qkv_s[..., D:2 * D], qkv_s[..., 2 * D:]
            o_s = _sattn(qs, ks, vs, cache_k[j], cache_v[j], page_table, sb_bias[j], q_scale)
            o_s = jnp.pad(o_s, ((0, 0), (n_pad_s, 0), (0, 0))).astype(BF16).reshape(1, RS, D)
            xs1, mix_s = xs, (o_s, wo, lng[0], lnb[0])
            outs["s_k"].append(ks.reshape(BD, T, HA, D // HA))
            outs["s_v"].append(vs.reshape(BD, T, HA, D // HA))
        xp = _mlp_ln(xp1, w1, w2, lng[1], lnb[1], alpha, rows_p, mix_p)
        xs = _mlp_ln(xs1, w1, w2, lng[1], lnb[1], alpha, RS, mix_s)
    y_prompt = xp[:, :SEQ]
    y_sample = xs.reshape(BD, SAMPLE_ROWS, D)[:, n_pad_s:]
    st = lambda name: jnp.stack(outs[name])
    return (y_prompt, y_sample, st("p_C"), st("p_n"), st("p_m"), st("p_conv"), st("p_k"), st("p_v"),
            st("s_C"), st("s_n"), st("s_m"), st("s_conv"), st("s_k"), st("s_v"))
```
